```python
import math
import jax, jax.numpy as jnp
from jax import lax
import numpy as np

D_MODEL = 4096
BATCH = 2
SEQ = 8192
DEPTH = 1

SSM_EXPAND = 2
D_SSM = SSM_EXPAND * D_MODEL
SSM_HEADDIM = 64
SSM_HEADS = D_SSM // SSM_HEADDIM
SSM_GROUPS = 8
SSM_STATE = 128
CONV_WIDTH = 4
CHUNK = 128
D_CONV = D_SSM + 2 * SSM_GROUPS * SSM_STATE

MLA_HEADS = 32
QK_NOPE = 128
QK_ROPE = 64
V_HEAD = 128
Q_LORA = 1024
KV_LORA = 512
D_ATTN = MLA_HEADS * V_HEAD
ROPE_THETA = 10000.0
Q_BLOCK = 128

EPS = 1e-6

IN_SIZES = (D_SSM, D_CONV, SSM_HEADS, Q_LORA, KV_LORA + QK_ROPE, D_ATTN, D_MODEL, D_MODEL)
D_IN = D_SSM + D_CONV + SSM_HEADS + Q_LORA + KV_LORA + QK_ROPE + D_ATTN + 2 * D_MODEL

kernel_name = "hybrid_ssd_mla_gated_merge"


def rmsnorm(x, w):
    xf = x.astype(jnp.float32)
    xf = xf * lax.rsqrt(jnp.mean(xf * xf, axis=-1, keepdims=True) + EPS)
    return xf.astype(x.dtype) * w


def causal_depthwise_conv(u, w, b):
    L = u.shape[1]
    up = jnp.pad(u, ((0, 0), (CONV_WIDTH - 1, 0), (0, 0)))
    out = b
    for k in range(CONV_WIDTH):
        out = out + up[:, k:k + L] * w[k]
    return out


def rope(x, pos):
    half = x.shape[-1] // 2
    inv_freq = ROPE_THETA ** (-jnp.arange(0, half, dtype=jnp.float32) / half)
    ang = pos.astype(jnp.float32)[:, None] * inv_freq[None, :]
    cos = jnp.cos(ang)[None, :, None, :]
    sin = jnp.sin(ang)[None, :, None, :]
    xf = x.astype(jnp.float32)
    x1, x2 = xf[..., :half], xf[..., half:]
    out = jnp.concatenate([x1 * cos - x2 * sin, x2 * cos + x1 * sin], axis=-1)
    return out.astype(x.dtype)


def ssd_chunked(x, dt, A, Bm, Cm):
    b, L = x.shape[:2]
    c = L // CHUNK
    r = SSM_HEADS // SSM_GROUPS
    G, N, P = SSM_GROUPS, SSM_STATE, SSM_HEADDIM
    xdt = (x.astype(jnp.float32) * dt[..., None]).reshape(b, c, CHUNK, G, r, P)
    a = (dt * A).reshape(b, c, CHUNK, G, r)
    Bc = Bm.astype(jnp.float32).reshape(b, c, CHUNK, G, N)
    Cc = Cm.astype(jnp.float32).reshape(b, c, CHUNK, G, N)
    a_cs = jnp.cumsum(a, axis=2)
    seg = a_cs[:, :, :, None] - a_cs[:, :, None]
    causal = jnp.tril(jnp.ones((CHUNK, CHUNK), dtype=bool))[None, None, :, :, None, None]
    decay = jnp.exp(jnp.where(causal, seg, -jnp.inf))
    cb = jnp.einsum("bclgn,bcsgn->bclsg", Cc, Bc)
    scores = cb[..., None] * decay
    y_diag = jnp.einsum("bclsgr,bcsgrp->bclgrp", scores, xdt)
    decay_states = jnp.exp(a_cs[:, :, -1:] - a_cs)
    states = jnp.einsum("bclgn,bclgrp->bcgrpn", Bc, xdt * decay_states[..., None])
    chunk_decay = jnp.exp(a_cs[:, :, -1])

    def step(h, inp):
        s, d = inp
        return h * d[..., None, None] + s, h

    h0 = jnp.zeros((b, G, r, P, N), jnp.float32)
    _, prev = lax.scan(step, h0, (jnp.moveaxis(states, 1, 0), jnp.moveaxis(chunk_decay, 1, 0)))
    prev = jnp.moveaxis(prev, 0, 1)
    y_off = jnp.einsum("bclgn,bcgrpn->bclgrp", Cc, prev) * jnp.exp(a_cs)[..., None]
    return (y_diag + y_off).reshape(b, L, SSM_HEADS, P)


def gated_group_rmsnorm(y, z, w):
    b, L, _ = y.shape
    g = (y.astype(jnp.float32) * jax.nn.silu(z.astype(jnp.float32))).reshape(b, L, SSM_GROUPS, D_SSM // SSM_GROUPS)
    g = g * lax.rsqrt(jnp.mean(g * g, axis=-1, keepdims=True) + EPS)
    return g.reshape(b, L, D_SSM).astype(y.dtype) * w


def mla(cq_raw, kv_raw, q_norm_w, w_uq, kv_norm_w, w_ukv, pos):
    b, L = cq_raw.shape[:2]
    cq = rmsnorm(cq_raw, q_norm_w)
    q = (cq @ w_uq).reshape(b, L, MLA_HEADS, QK_NOPE + QK_ROPE)
    q_nope = q[..., :QK_NOPE]
    q_rope = rope(q[..., QK_NOPE:], pos)
    ckv = rmsnorm(kv_raw[..., :KV_LORA], kv_norm_w)
    k_rope = rope(kv_raw[..., KV_LORA:][:, :, None, :], pos)[:, :, 0]
    kv = (ckv @ w_ukv).reshape(b, L, MLA_HEADS, QK_NOPE + V_HEAD)
    k_nope, v = kv[..., :QK_NOPE], kv[..., QK_NOPE:]
    scale = 1.0 / math.sqrt(QK_NOPE + QK_ROPE)
    nb = L // Q_BLOCK
    qn = q_nope.reshape(b, nb, Q_BLOCK, MLA_HEADS, QK_NOPE).swapaxes(0, 1)
    qr = q_rope.reshape(b, nb, Q_BLOCK, MLA_HEADS, QK_ROPE).swapaxes(0, 1)
    kpos = pos

    def block(args):
        qn_i, qr_i, i = args
        s = jnp.einsum("bqhd,bkhd->bhqk", qn_i, k_nope) + jnp.einsum("bqhr,bkr->bhqk", qr_i, k_rope)
        s = s.astype(jnp.float32) * scale
        qpos = lax.dynamic_slice_in_dim(pos, i * Q_BLOCK, Q_BLOCK)
        mask = kpos[None, :] <= qpos[:, None]
        s = jnp.where(mask[None, None], s, -jnp.inf)
        p = jax.nn.softmax(s, axis=-1).astype(v.dtype)
        return jnp.einsum("bhqk,bkhd->bqhd", p, v)

    o = lax.map(block, (qn, qr, jnp.arange(nb)))
    return o.swapaxes(0, 1).reshape(b, L, D_ATTN)


def setup_inputs(seed: int = 0) -> dict:
    key = jax.random.key(seed)
    ks = jax.random.split(key, 20)
    f32 = jnp.float32

    def lin(k, fan_in, fan_out):
        return jax.random.normal(k, (DEPTH, fan_in, fan_out), f32) * fan_in ** -0.5

    def gain(k, n):
        return 1.0 + 0.02 * jax.random.normal(k, (DEPTH, n), f32)

    dt0 = jnp.exp(jax.random.uniform(ks[5], (DEPTH, SSM_HEADS), f32, math.log(1e-3), math.log(1e-1)))
    dt_bias = dt0 + jnp.log(-jnp.expm1(-dt0))
    a_log = jnp.log(jax.random.uniform(ks[6], (DEPTH, SSM_HEADS), f32, 1.0, 16.0))
    return {
        "x": jax.random.normal(ks[0], (BATCH, SEQ, D_MODEL), f32),
        "norm_in_w": gain(ks[1], D_MODEL),
        "w_in": lin(ks[2], D_MODEL, D_IN),
        "conv_w": jax.random.normal(ks[3], (DEPTH, CONV_WIDTH, D_CONV), f32) * CONV_WIDTH ** -0.5,
        "conv_b": 0.01 * jax.random.normal(ks[4], (DEPTH, D_CONV), f32),
        "dt_bias": dt_bias,
        "a_log": a_log,
        "d_skip": 1.0 + 0.1 * jax.random.normal(ks[7], (DEPTH, SSM_HEADS), f32),
        "ssm_norm_w": gain(ks[8], D_SSM),
        "q_norm_w": gain(ks[9], Q_LORA),
        "w_uq": lin(ks[10], Q_LORA, MLA_HEADS * (QK_NOPE + QK_ROPE)),
        "kv_norm_w": gain(ks[11], KV_LORA),
        "w_ukv": lin(ks[12], KV_LORA, MLA_HEADS * (QK_NOPE + V_HEAD)),
        "w_branch_ssm": lin(ks[13], D_SSM, D_MODEL),
        "w_branch_attn": lin(ks[14], D_ATTN, D_MODEL),
        "w_out": lin(ks[15], D_MODEL, D_MODEL),
        "norm_final_w": 1.0 + 0.02 * jax.random.normal(ks[16], (D_MODEL,), f32),
    }


def reference(x, norm_in_w, w_in, conv_w, conv_b, dt_bias, a_log, d_skip, ssm_norm_w,
              q_norm_w, w_uq, kv_norm_w, w_ukv, w_branch_ssm, w_branch_attn, w_out, norm_final_w):
    b, L, _ = x.shape
    pos = jnp.arange(L, dtype=jnp.int32)
    splits = tuple(int(s) for s in np.cumsum(IN_SIZES)[:-1])
    h = x
    for layer in range(DEPTH):
        u = rmsnorm(h, norm_in_w[layer])
        proj = u @ w_in[layer]
        z, xbc, dt_raw, cq_raw, kv_raw, g_attn, gate_ssm, gate_attn = jnp.split(proj, splits, axis=-1)

        xbc = jax.nn.silu(causal_depthwise_conv(xbc, conv_w[layer], conv_b[layer]))
        xs = xbc[..., :D_SSM]
        Bm = xbc[..., D_SSM:D_SSM + SSM_GROUPS * SSM_STATE].reshape(b, L, SSM_GROUPS, SSM_STATE)
        Cm = xbc[..., D_SSM + SSM_GROUPS * SSM_STATE:].reshape(b, L, SSM_GROUPS, SSM_STATE)
        dt = jax.nn.softplus(dt_raw.astype(jnp.float32) + dt_bias[layer].astype(jnp.float32))
        A = -jnp.exp(a_log[layer].astype(jnp.float32))
        xh = xs.reshape(b, L, SSM_HEADS, SSM_HEADDIM)
        y = ssd_chunked(xh, dt, A, Bm, Cm) + xh.astype(jnp.float32) * d_skip[layer].astype(jnp.float32)[:, None]
        y_ssm = gated_group_rmsnorm(y.astype(x.dtype).reshape(b, L, D_SSM), z, ssm_norm_w[layer])

        o = mla(cq_raw, kv_raw, q_norm_w[layer], w_uq[layer], kv_norm_w[layer], w_ukv[layer], pos)
        y_attn = o * jax.nn.silu(g_attn)

        merged = (jax.nn.sigmoid(gate_ssm) * (y_ssm @ w_branch_ssm[layer])
                  + jax.nn.sigmoid(gate_attn) * (y_attn @ w_branch_attn[layer]))
        h = h + merged @ w_out[layer]
    return rmsnorm(h, norm_final_w)
```

```python
import functools
import math

import jax
import jax.numpy as jnp
from jax import lax
from jax.experimental import pallas as pl
from jax.experimental.pallas import tpu as pltpu

F32 = jnp.float32
BF16 = jnp.bfloat16

EPS = 1e-6
SSM_HEADDIM = 64
SSM_STATE = 128
CONV_WIDTH = 4
CHUNK = 128
QK_NOPE = 128
QK_ROPE = 64
V_HEAD = 128
ROPE_THETA = 10000.0

LANES = 128
SUBLANES = 8
VMEM_LIMIT_BYTES = 56 * 1024 * 1024
QK_PAD = 2 * LANES
MASK_VALUE = -1e30


def _params(*sem):
    return pltpu.CompilerParams(dimension_semantics=sem, vmem_limit_bytes=VMEM_LIMIT_BYTES)


def _sigmoid(x):
    return 1.0 / (1.0 + jnp.exp(-x))


def _silu(x):
    return x * _sigmoid(x)


def _rmsnorm_kernel(x_ref, w_ref, o_ref):
    x = x_ref[...].astype(F32)
    ms = jnp.mean(x * x, axis=-1, keepdims=True)
    o_ref[...] = ((x * lax.rsqrt(ms + EPS)) * w_ref[...]).astype(o_ref.dtype)


def _rmsnorm(x, w, out_dtype, bm, name):
    m, d = x.shape
    return pl.pallas_call(
        _rmsnorm_kernel,
        grid=(m // bm,),
        in_specs=[pl.BlockSpec((bm, d), lambda i: (i, 0)), pl.BlockSpec((1, d), lambda i: (0, 0))],
        out_specs=pl.BlockSpec((bm, d), lambda i: (i, 0)),
        out_shape=jax.ShapeDtypeStruct((m, d), out_dtype),
        compiler_params=_params("parallel"),
        name=name,
    )(x, w.reshape(1, d))


def _matmul_kernel(a_ref, b_ref, o_ref):
    o_ref[...] = jnp.dot(a_ref[...], b_ref[...], preferred_element_type=F32).astype(o_ref.dtype)


def _matmul(a, b, out_dtype, bm, bn, name):
    m, k = a.shape
    n = b.shape[1]
    return pl.pallas_call(
        _matmul_kernel,
        grid=(n // bn, m // bm),
        in_specs=[pl.BlockSpec((bm, k), lambda j, i: (i, 0)), pl.BlockSpec((k, bn), lambda j, i: (0, j))],
        out_specs=pl.BlockSpec((bm, bn), lambda j, i: (i, j)),
        out_shape=jax.ShapeDtypeStruct((m, n), out_dtype),
        compiler_params=_params("parallel", "parallel"),
        name=name,
    )(a, b)


def _dt_kernel(u_ref, w_ref, bias_ref, alog_ref, dt_ref, acs_ref, acst_ref):
    raw = jnp.dot(u_ref[...], w_ref[...], preferred_element_type=F32) + bias_ref[...]
    dt = jnp.maximum(raw, 0.0) + jnp.log1p(jnp.exp(-jnp.abs(raw)))
    a = dt * (-jnp.exp(alog_ref[...]))
    li = lax.broadcasted_iota(jnp.int32, (CHUNK, CHUNK), 0)
    ki = lax.broadcasted_iota(jnp.int32, (CHUNK, CHUNK), 1)
    tril = (ki <= li).astype(F32)
    acs = jnp.dot(tril, a, precision=lax.Precision.HIGHEST, preferred_element_type=F32)
    dt_ref[...] = dt
    acs_ref[...] = acs
    acst_ref[0] = acs.T


def _dt_prep(u, w_dt, dt_bias, a_log):
    m, d = u.shape
    h = w_dt.shape[1]
    nc = m // CHUNK
    return pl.pallas_call(
        _dt_kernel,
        grid=(nc,),
        in_specs=[
            pl.BlockSpec((CHUNK, d), lambda i: (i, 0)),
            pl.BlockSpec((d, h), lambda i: (0, 0)),
            pl.BlockSpec((1, h), lambda i: (0, 0)),
            pl.BlockSpec((1, h), lambda i: (0, 0)),
        ],
        out_specs=[
            pl.BlockSpec((CHUNK, h), lambda i: (i, 0)),
            pl.BlockSpec((CHUNK, h), lambda i: (i, 0)),
            pl.BlockSpec((1, h, CHUNK), lambda i: (i, 0, 0)),
        ],
        out_shape=[
            jax.ShapeDtypeStruct((m, h), F32),
            jax.ShapeDtypeStruct((m, h), F32),
            jax.ShapeDtypeStruct((nc, h, CHUNK), F32),
        ],
        compiler_params=_params("parallel"),
        name="dt_prep",
    )(u, w_dt, dt_bias.reshape(1, h), a_log.reshape(1, h))


def _exact_expand(v, onehot3_ref):
    hi = v.astype(BF16)
    r1 = v - hi.astype(F32)
    mid = r1.astype(BF16)
    lo = (r1 - mid.astype(F32)).astype(BF16)
    return jnp.dot(jnp.concatenate([hi, mid, lo], axis=1), onehot3_ref[...], preferred_element_type=F32)


def _ssd_kernel(xs_ref, b_ref, c_ref, z_ref, dt_ref, acs_ref, acst_ref,
                cwx_ref, cwb_ref, cwc_ref, cbx_ref, cbb_ref, cbc_ref,
                rexp_ref, rexp3_ref, eexp3_ref, dskip_ref, nw_ref,
                y_ref, pad_ref, state_ref, ydiag_ref, *, r):
    gw = r * SSM_HEADDIM
    n = SSM_STATE
    halo = SUBLANES

    @pl.when(pl.program_id(2) == 0)
    def _():
        pad_ref[0:halo, :] = jnp.zeros((halo, gw + 2 * n), F32)
        state_ref[...] = jnp.zeros_like(state_ref)

    pad_ref[halo:halo + CHUNK, 0:gw] = xs_ref[...].astype(F32)
    pad_ref[halo:halo + CHUNK, gw:gw + n] = b_ref[...].astype(F32)
    pad_ref[halo:halo + CHUNK, gw + n:gw + 2 * n] = c_ref[...].astype(F32)

    def conv_silu(col0, width, w_ref, bias_ref):
        acc = bias_ref[...]
        for k in range(CONV_WIDTH):
            row0 = halo - (CONV_WIDTH - 1) + k
            acc = acc + pad_ref[row0:row0 + CHUNK, col0:col0 + width] * w_ref[k:k + 1, :]
        return _silu(acc)

    xc = conv_silu(0, gw, cwx_ref, cbx_ref)
    bc = conv_silu(gw, n, cwb_ref, cbb_ref)
    cc = conv_silu(gw + n, n, cwc_ref, cbc_ref)
    pad_ref[0:halo, :] = pad_ref[CHUNK:CHUNK + halo, :]

    cb16 = cc.astype(BF16)
    bb16 = bc.astype(BF16)
    cbm = lax.dot_general(cb16, bb16, (((1,), (1,)), ((), ())), preferred_element_type=F32)

    acs = acs_ref[...]
    acs_last = acs[CHUNK - 1:CHUNK, :]
    per_head = jnp.concatenate([dt_ref[...], jnp.exp(acs), jnp.exp(acs_last - acs)], axis=0)
    ex = jnp.dot(per_head.astype(BF16), rexp_ref[...], preferred_element_type=F32)
    dt_e = ex[0:CHUNK]
    eacs_e = ex[CHUNK:2 * CHUNK]
    dst_e = ex[2 * CHUNK:3 * CHUNK]
    last_e = _exact_expand(jnp.broadcast_to(acs_last, (SUBLANES, acs.shape[1])), rexp3_ref)
    cd_e = jnp.exp(last_e[0:1, :])
    col_all = _exact_expand(acs, eexp3_ref)

    x32 = xc * dt_e
    x16 = x32.astype(BF16)
    li = lax.broadcasted_iota(jnp.int32, (CHUNK, CHUNK), 0)
    si = lax.broadcasted_iota(jnp.int32, (CHUNK, CHUNK), 1)
    causal = si <= li
    low_half = si < SSM_HEADDIM
    for p in range(r // 2):
        xp = x16[:, p * LANES:(p + 1) * LANES]
        ys = []
        for q in range(2):
            j = 2 * p + q
            seg = col_all[:, j * LANES:(j + 1) * LANES] - acst_ref[0, j:j + 1, :]
            mj = (jnp.exp(jnp.where(causal, seg, -jnp.inf)) * cbm).astype(BF16)
            ys.append(jnp.dot(mj, xp, preferred_element_type=F32))
        ydiag_ref[:, p * LANES:(p + 1) * LANES] = jnp.where(low_half, ys[0], ys[1])

    st = state_ref[...]
    y_off = jnp.dot(cb16, st.astype(BF16), preferred_element_type=F32) * eacs_e
    xd16 = (x32 * dst_e).astype(BF16)
    state_ref[...] = st * cd_e + jnp.dot(bc.T.astype(BF16), xd16, preferred_element_type=F32)

    y = ydiag_ref[...] + y_off + xc * dskip_ref[...]
    gv = y * _silu(z_ref[...].astype(F32))
    ms = jnp.mean(gv * gv, axis=-1, keepdims=True)
    y_ref[...] = ((gv * lax.rsqrt(ms + EPS)) * nw_ref[...]).astype(y_ref.dtype)


def _ssd(proj, off, dt, acs, acst, conv_w, conv_b, d_skip, ssm_norm_w, batch, seq, groups, d_ssm):
    m = proj.shape[0]
    heads = dt.shape[1]
    r = heads // groups
    gw = r * SSM_HEADDIM
    n = SSM_STATE
    nc = seq // CHUNK
    assert heads == LANES and r % 2 == 0 and gw % LANES == 0 and seq % CHUNK == 0

    head_of_col = jnp.arange(d_ssm, dtype=jnp.int32) // SSM_HEADDIM
    rexp = (jnp.arange(heads, dtype=jnp.int32)[:, None] == head_of_col[None, :]).astype(BF16)
    head_of_tile = jnp.arange(heads * LANES, dtype=jnp.int32) // LANES
    eexp = (jnp.arange(heads, dtype=jnp.int32)[:, None] == head_of_tile[None, :]).astype(BF16)
    rexp3 = jnp.concatenate([rexp] * 3, axis=0)
    eexp3 = jnp.concatenate([eexp] * 3, axis=0)
    dskip_e = jnp.repeat(d_skip.astype(F32), SSM_HEADDIM).reshape(1, d_ssm)
    cw = conv_w.astype(F32)
    cb = conv_b.astype(F32).reshape(1, -1)

    def row(b, g, c):
        return b * nc + c

    xs0, b0, c0, z0 = off["xs"] // gw, off["B"] // n, off["C"] // n, off["z"] // gw
    cwb0 = d_ssm // n
    in_specs = [
        pl.BlockSpec((CHUNK, gw), lambda b, g, c: (row(b, g, c), xs0 + g)),
        pl.BlockSpec((CHUNK, n), lambda b, g, c: (row(b, g, c), b0 + g)),
        pl.BlockSpec((CHUNK, n), lambda b, g, c: (row(b, g, c), c0 + g)),
        pl.BlockSpec((CHUNK, gw), lambda b, g, c: (row(b, g, c), z0 + g)),
        pl.BlockSpec((CHUNK, heads), lambda b, g, c: (row(b, g, c), 0)),
        pl.BlockSpec((CHUNK, heads), lambda b, g, c: (row(b, g, c), 0)),
        pl.BlockSpec((1, r, CHUNK), lambda b, g, c: (row(b, g, c), g, 0)),
        pl.BlockSpec((CONV_WIDTH, gw), lambda b, g, c: (0, g)),
        pl.BlockSpec((CONV_WIDTH, n), lambda b, g, c: (0, cwb0 + g)),
        pl.BlockSpec((CONV_WIDTH, n), lambda b, g, c: (0, cwb0 + groups + g)),
        pl.BlockSpec((1, gw), lambda b, g, c: (0, g)),
        pl.BlockSpec((1, n), lambda b, g, c: (0, cwb0 + g)),
        pl.BlockSpec((1, n), lambda b, g, c: (0, cwb0 + groups + g)),
        pl.BlockSpec((heads, gw), lambda b, g, c: (0, g)),
        pl.BlockSpec((3 * heads, gw), lambda b, g, c: (0, g)),
        pl.BlockSpec((3 * heads, r * LANES), lambda b, g, c: (0, g)),
        pl.BlockSpec((1, gw), lambda b, g, c: (0, g)),
        pl.BlockSpec((1, gw), lambda b, g, c: (0, g)),
    ]
    return pl.pallas_call(
        functools.partial(_ssd_kernel, r=r),
        grid=(batch, groups, nc),
        in_specs=in_specs,
        out_specs=pl.BlockSpec((CHUNK, gw), lambda b, g, c: (row(b, g, c), g)),
        out_shape=jax.ShapeDtypeStruct((m, d_ssm), BF16),
        scratch_shapes=[
            pltpu.VMEM((CHUNK + SUBLANES, gw + 2 * n), F32),
            pltpu.VMEM((n, gw), F32),
            pltpu.VMEM((CHUNK, gw), F32),
        ],
        compiler_params=_params("parallel", "parallel", "arbitrary"),
        name="ssd_scan",
    )(proj, proj, proj, proj, dt, acs, acst, cw, cw, cw, cb, cb, cb,
      rexp, rexp3, eexp3, dskip_e, ssm_norm_w.astype(F32).reshape(1, d_ssm))


def _normed(x_ref, w_ref):
    x = x_ref[...].astype(F32)
    ms = jnp.mean(x * x, axis=-1, keepdims=True)
    return ((x * lax.rsqrt(ms + EPS)) * w_ref[...]).astype(BF16)


def _q_kernel(cq_ref, nw_ref, wq_ref, wsw_ref, cs1_ref, cs2_ref, o_ref, *, hb, scale):
    cqn = _normed(cq_ref, nw_ref)
    q = jnp.dot(cqn, wq_ref[...], preferred_element_type=F32)
    qs = jnp.dot(cqn, wsw_ref[...], preferred_element_type=F32)
    cs1 = cs1_ref[...]
    cs2 = cs2_ref[...]
    for h in range(hb):
        c0 = h * QK_PAD
        o_ref[:, c0:c0 + LANES] = (q[:, c0:c0 + LANES] * scale).astype(o_ref.dtype)
        roped = q[:, c0 + LANES:c0 + QK_PAD] * cs1 + qs[:, h * LANES:(h + 1) * LANES] * cs2
        o_ref[:, c0 + LANES:c0 + QK_PAD] = (roped * scale).astype(o_ref.dtype)


def _kv_kernel(kvc_ref, kr_ref, krsw_ref, nw_ref, w_ref, cs1_ref, cs2_ref, k_ref, v_ref, *, hb):
    ckv = _normed(kvc_ref, nw_ref)
    kv = jnp.dot(ckv, w_ref[...], preferred_element_type=F32)
    roped = (kr_ref[...].astype(F32) * cs1_ref[...] + krsw_ref[...].astype(F32) * cs2_ref[...]).astype(k_ref.dtype)
    for h in range(hb):
        c0 = h * QK_PAD
        k_ref[:, c0:c0 + LANES] = kv[:, c0:c0 + LANES].astype(k_ref.dtype)
        k_ref[:, c0 + LANES:c0 + QK_PAD] = roped
        v_ref[:, h * V_HEAD:(h + 1) * V_HEAD] = kv[:, c0 + LANES:c0 + QK_PAD].astype(v_ref.dtype)


def _rope_tables(seq):
    half = QK_ROPE // 2
    inv_freq = ROPE_THETA ** (-jnp.arange(0, half, dtype=F32) / half)
    ang = jnp.arange(seq, dtype=jnp.int32).astype(F32)[:, None] * inv_freq[None, :]
    cos, sin = jnp.cos(ang), jnp.sin(ang)
    zeros = jnp.zeros((seq, LANES - QK_ROPE), F32)
    return jnp.concatenate([cos, cos, zeros], axis=1), jnp.concatenate([-sin, sin, zeros], axis=1)


def _q_proj(proj, off, q_norm_w, w_uq, cs1, cs2, seq, heads, bm, hb):
    m = proj.shape[0]
    ql = w_uq.shape[0]
    w = w_uq.reshape(ql, heads, QK_NOPE + QK_ROPE)
    nope, rp = w[:, :, :QK_NOPE], w[:, :, QK_NOPE:]
    half = QK_ROPE // 2
    zpad = jnp.zeros((ql, heads, LANES - QK_ROPE), w.dtype)
    wq = jnp.concatenate([nope, rp, zpad], axis=2).reshape(ql, heads * QK_PAD).astype(BF16)
    wsw = jnp.concatenate([rp[:, :, half:], rp[:, :, :half], zpad], axis=2).reshape(ql, heads * LANES).astype(BF16)
    nb = seq // bm
    scale = 1.0 / math.sqrt(QK_NOPE + QK_ROPE)
    cq0 = off["cq"] // ql
    return pl.pallas_call(
        functools.partial(_q_kernel, hb=hb, scale=scale),
        grid=(heads // hb, m // bm),
        in_specs=[
            pl.BlockSpec((bm, ql), lambda j, i: (i, cq0)),
            pl.BlockSpec((1, ql), lambda j, i: (0, 0)),
            pl.BlockSpec((ql, hb * QK_PAD), lambda j, i: (0, j)),
            pl.BlockSpec((ql, hb * LANES), lambda j, i: (0, j)),
            pl.BlockSpec((bm, LANES), lambda j, i: (i % nb, 0)),
            pl.BlockSpec((bm, LANES), lambda j, i: (i % nb, 0)),
        ],
        out_specs=pl.BlockSpec((bm, hb * QK_PAD), lambda j, i: (i, j)),
        out_shape=jax.ShapeDtypeStruct((m, heads * QK_PAD), BF16),
        compiler_params=_params("parallel", "parallel"),
        name="q_proj",
    )(proj, q_norm_w.astype(F32).reshape(1, ql), wq, wsw, cs1, cs2)


def _kv_proj(proj, off, kv_norm_w, w_ukv, cs1, cs2, seq, heads, bm, hb):
    m = proj.shape[0]
    kvl = w_ukv.shape[0]
    nb = seq // bm
    c0, kr0, krsw0 = off["kvc"] // kvl, off["kr"] // LANES, off["krsw"] // LANES
    return pl.pallas_call(
        functools.partial(_kv_kernel, hb=hb),
        grid=(heads // hb, m // bm),
        in_specs=[
            pl.BlockSpec((bm, kvl), lambda j, i: (i, c0)),
            pl.BlockSpec((bm, LANES), lambda j, i: (i, kr0)),
            pl.BlockSpec((bm, LANES), lambda j, i: (i, krsw0)),
            pl.BlockSpec((1, kvl), lambda j, i: (0, 0)),
            pl.BlockSpec((kvl, hb * QK_PAD), lambda j, i: (0, j)),
            pl.BlockSpec((bm, LANES), lambda j, i: (i % nb, 0)),
            pl.BlockSpec((bm, LANES), lambda j, i: (i % nb, 0)),
        ],
        out_specs=[
            pl.BlockSpec((bm, hb * QK_PAD), lambda j, i: (i, j)),
            pl.BlockSpec((bm, hb * V_HEAD), lambda j, i: (i, j)),
        ],
        out_shape=[
            jax.ShapeDtypeStruct((m, heads * QK_PAD), BF16),
            jax.ShapeDtypeStruct((m, heads * V_HEAD), BF16),
        ],
        compiler_params=_params("parallel", "parallel"),
        name="kv_proj",
    )(proj, proj, proj, kv_norm_w.astype(F32).reshape(1, kvl), w_ukv.astype(BF16), cs1, cs2)


def _attn_kernel(q_ref, k_ref, v_ref, g_ref, o_ref, m_ref, l_ref, acc_ref, *, blk):
    qi = pl.program_id(2)
    m_ref[...] = jnp.full(m_ref.shape, MASK_VALUE, F32)
    l_ref[...] = jnp.zeros_like(l_ref)
    acc_ref[...] = jnp.zeros_like(acc_ref)
    q = q_ref[...]

    def step(ki, masked):
        start = pl.multiple_of(ki * blk, blk)
        k = k_ref[pl.ds(start, blk), :]
        v = v_ref[pl.ds(start, blk), :]
        s = lax.dot_general(q, k, (((1,), (1,)), ((), ())), preferred_element_type=F32)
        if masked:
            ri = lax.broadcasted_iota(jnp.int32, (blk, blk), 0)
            ci = lax.broadcasted_iota(jnp.int32, (blk, blk), 1)
            s = jnp.where(ci <= ri, s, MASK_VALUE)
        m_prev = m_ref[...]
        m_next = jnp.maximum(m_prev, jnp.max(s, axis=1, keepdims=True))
        p = jnp.exp(s - jnp.concatenate([m_next] * (blk // LANES), axis=1))
        alpha = jnp.exp(m_prev - m_next)
        l_ref[...] = alpha * l_ref[...] + jnp.sum(p, axis=1, keepdims=True)
        acc_ref[...] = alpha * acc_ref[...] + jnp.dot(p.astype(v.dtype), v, preferred_element_type=F32)
        m_ref[...] = m_next

    def body(ki, carry):
        step(ki, False)
        return carry

    lax.fori_loop(0, qi, body, 0)
    step(qi, True)
    g = g_ref[...].astype(F32)
    o_ref[...] = ((acc_ref[...] / l_ref[...]) * _silu(g)).astype(o_ref.dtype)


def _attention(q, k, v, proj, off, batch, seq, heads, blk):
    m = q.shape[0]
    nq = seq // blk
    g0 = off["g_attn"] // V_HEAD
    return pl.pallas_call(
        functools.partial(_attn_kernel, blk=blk),
        grid=(batch, heads, nq),
        in_specs=[
            pl.BlockSpec((blk, QK_PAD), lambda b, h, i: (b * nq + i, h)),
            pl.BlockSpec((seq, QK_PAD), lambda b, h, i: (b, h)),
            pl.BlockSpec((seq, V_HEAD), lambda b, h, i: (b, h)),
            pl.BlockSpec((blk, V_HEAD), lambda b, h, i: (b * nq + i, g0 + h)),
        ],
        out_specs=pl.BlockSpec((blk, V_HEAD), lambda b, h, i: (b * nq + i, h)),
        out_shape=jax.ShapeDtypeStruct((m, heads * V_HEAD), BF16),
        scratch_shapes=[
            pltpu.VMEM((blk, LANES), F32),
            pltpu.VMEM((blk, LANES), F32),
            pltpu.VMEM((blk, V_HEAD), F32),
        ],
        compiler_params=_params("parallel", "parallel", "arbitrary"),
        name="mla_attention",
    )(q, k, v, proj)


def _merge_kernel(ys_ref, ya_ref, ws_ref, wa_ref, gs_ref, ga_ref, o_ref):
    a = jnp.dot(ys_ref[...], ws_ref[...], preferred_element_type=F32)
    b = jnp.dot(ya_ref[...], wa_ref[...], preferred_element_type=F32)
    o = _sigmoid(gs_ref[...].astype(F32)) * a + _sigmoid(ga_ref[...].astype(F32)) * b
    o_ref[...] = o.astype(o_ref.dtype)


def _merge(y_ssm, y_attn, w_s, w_a, proj, off, bm, bn):
    m, ds = y_ssm.shape
    da = y_attn.shape[1]
    d = w_s.shape[1]
    gs0, ga0 = off["gate_ssm"] // bn, off["gate_attn"] // bn
    return pl.pallas_call(
        _merge_kernel,
        grid=(d // bn, m // bm),
        in_specs=[
            pl.BlockSpec((bm, ds), lambda j, i: (i, 0)),
            pl.BlockSpec((bm, da), lambda j, i: (i, 0)),
            pl.BlockSpec((ds, bn), lambda j, i: (0, j)),
            pl.BlockSpec((da, bn), lambda j, i: (0, j)),
            pl.BlockSpec((bm, bn), lambda j, i: (i, gs0 + j)),
            pl.BlockSpec((bm, bn), lambda j, i: (i, ga0 + j)),
        ],
        out_specs=pl.BlockSpec((bm, bn), lambda j, i: (i, j)),
        out_shape=jax.ShapeDtypeStruct((m, d), BF16),
        compiler_params=_params("parallel", "parallel"),
        name="merge_proj",
    )(y_ssm, y_attn, w_s, w_a, proj, proj)


def _out_kernel(a_ref, b_ref, x_ref, o_ref):
    o_ref[...] = x_ref[...] + jnp.dot(a_ref[...], b_ref[...], preferred_element_type=F32)


def _out_proj(merged, w_out, x, bm, bn):
    m, k = merged.shape
    n = w_out.shape[1]
    return pl.pallas_call(
        _out_kernel,
        grid=(n // bn, m // bm),
        in_specs=[
            pl.BlockSpec((bm, k), lambda j, i: (i, 0)),
            pl.BlockSpec((k, bn), lambda j, i: (0, j)),
            pl.BlockSpec((bm, bn), lambda j, i: (i, j)),
        ],
        out_specs=pl.BlockSpec((bm, bn), lambda j, i: (i, j)),
        out_shape=jax.ShapeDtypeStruct((m, n), F32),
        compiler_params=_params("parallel", "parallel"),
        name="out_proj",
    )(merged, w_out, x)


def _pack_in_proj(w_in, d_ssm, groups, heads_ssm, q_lora, kv_lora, d_attn, d_model, gw, bn_merge, bn_in):
    gn = groups * SSM_STATE
    sizes = (d_ssm, d_ssm + 2 * gn, heads_ssm, q_lora, kv_lora + QK_ROPE, d_attn, d_model, d_model)
    starts = [0]
    for s in sizes[:-1]:
        starts.append(starts[-1] + s)
    z0, xbc0, dt0, cq0, kv0, ga0, gs0, gt0 = starts
    half = QK_ROPE // 2
    d = w_in.shape[0]
    w_kr = w_in[:, kv0 + kv_lora:kv0 + kv_lora + QK_ROPE]
    lane_pad = jnp.zeros((d, LANES - QK_ROPE), w_in.dtype)
    pieces = [
        ("z", w_in[:, z0:z0 + d_ssm], gw),
        ("xs", w_in[:, xbc0:xbc0 + d_ssm], gw),
        ("B", w_in[:, xbc0 + d_ssm:xbc0 + d_ssm + gn], SSM_STATE),
        ("C", w_in[:, xbc0 + d_ssm + gn:xbc0 + d_ssm + 2 * gn], SSM_STATE),
        ("cq", w_in[:, cq0:cq0 + q_lora], q_lora),
        ("kvc", w_in[:, kv0:kv0 + kv_lora], kv_lora),
        ("kr", jnp.concatenate([w_kr, lane_pad], axis=1), LANES),
        ("krsw", jnp.concatenate([w_kr[:, half:], w_kr[:, :half], lane_pad], axis=1), LANES),
        ("g_attn", w_in[:, ga0:ga0 + d_attn], V_HEAD),
        ("gate_ssm", w_in[:, gs0:gs0 + d_model], bn_merge),
        ("gate_attn", w_in[:, gt0:gt0 + d_model], bn_merge),
    ]
    off = {}
    cols = []
    pos = 0
    for name, w, align in pieces:
        start = -(-pos // align) * align
        if start > pos:
            cols.append(jnp.zeros((d, start - pos), BF16))
        off[name] = start
        cols.append(w.astype(BF16))
        pos = start + w.shape[1]
    total = -(-pos // bn_in) * bn_in
    if total > pos:
        cols.append(jnp.zeros((d, total - pos), BF16))
    return jnp.concatenate(cols, axis=1), w_in[:, dt0:dt0 + heads_ssm].astype(BF16), off


def kernel(x, norm_in_w, w_in, conv_w, conv_b, dt_bias, a_log, d_skip, ssm_norm_w, q_norm_w, w_uq,
           kv_norm_w, w_ukv, w_branch_ssm, w_branch_attn, w_out, norm_final_w):
    batch, seq, d_model = x.shape
    depth = w_in.shape[0]
    m = batch * seq
    d_ssm = w_branch_ssm.shape[1]
    d_attn = w_branch_attn.shape[1]
    heads_ssm = dt_bias.shape[1]
    groups = (conv_w.shape[2] - d_ssm) // (2 * SSM_STATE)
    gw = (heads_ssm // groups) * SSM_HEADDIM
    q_lora, kv_lora = w_uq.shape[1], w_ukv.shape[1]
    heads = d_attn // V_HEAD
    assert heads_ssm * SSM_HEADDIM == d_ssm and w_uq.shape[2] == heads * (QK_NOPE + QK_ROPE)

    bm_norm = min(256, m)
    bm_in, bn_in = min(1024, m), 1024
    bm_qkv, hb_qkv = min(512, seq), 8
    blk_attn = min(512, seq)
    bm_merge, bn_merge = min(512, m), 512
    bm_out, bn_out = min(1024, m), 1024

    cs1, cs2 = _rope_tables(seq)
    h = x.reshape(m, d_model)
    for layer in range(depth):
        w_pack, w_dt, off = _pack_in_proj(w_in[layer], d_ssm, groups, heads_ssm, q_lora, kv_lora, d_attn,
                                          d_model, gw, bn_merge, bn_in)
        u = _rmsnorm(h, norm_in_w[layer], BF16, bm_norm, "norm_in")
        proj = _matmul(u, w_pack, BF16, bm_in, bn_in, "in_proj")

        dt, acs, acst = _dt_prep(u, w_dt, dt_bias[layer].astype(F32), a_log[layer].astype(F32))
        y_ssm = _ssd(proj, off, dt, acs, acst, conv_w[layer], conv_b[layer], d_skip[layer],
                     ssm_norm_w[layer], batch, seq, groups, d_ssm)

        q = _q_proj(proj, off, q_norm_w[layer], w_uq[layer], cs1, cs2, seq, heads, bm_qkv, hb_qkv)
        k, v = _kv_proj(proj, off, kv_norm_w[layer], w_ukv[layer], cs1, cs2, seq, heads, bm_qkv, hb_qkv)
        y_attn = _attention(q, k, v, proj, off, batch, seq, heads, blk_attn)

        merged = _merge(y_ssm, y_attn, w_branch_ssm[layer].astype(BF16), w_branch_attn[layer].astype(BF16),
                        proj, off, bm_merge, bn_merge)
        h = _out_proj(merged, w_out[layer].astype(BF16), h, bm_out, bn_out)
    out = _rmsnorm(h, norm_final_w, x.dtype, bm_norm, "norm_final")
    return out.reshape(batch, seq, d_model)
```

```python
import functools
import math

import jax
import jax.numpy as jnp
from jax import lax
from jax.experimental import pallas as pl
from jax.experimental.pallas import tpu as pltpu

F32 = jnp.float32
BF16 = jnp.bfloat16

EPS = 1e-6
SSM_HEADDIM = 64
SSM_STATE = 128
CONV_WIDTH = 4
CHUNK = 128
QK_NOPE = 128
QK_ROPE = 64
V_HEAD = 128
ROPE_THETA = 10000.0

LANES = 128
SUBLANES = 8
PACKED_ROWS = 16
VMEM_LIMIT_BYTES = 56 * 1024 * 1024
QK_PAD = 2 * LANES
V_PAD = 2 * LANES
MASK_VALUE = -1e30
LOG2E = math.log2(math.e)


def _params(*sem):
    return pltpu.CompilerParams(dimension_semantics=sem, vmem_limit_bytes=VMEM_LIMIT_BYTES)


def _sigmoid(x):
    return 0.5 + 0.5 * jnp.tanh(0.5 * x)


def _silu(x):
    h = 0.5 * x
    return h + h * jnp.tanh(h)


def _rmsnorm_kernel(x_ref, w_ref, o_ref):
    x = x_ref[...].astype(F32)
    ms = jnp.mean(x * x, axis=-1, keepdims=True)
    o_ref[...] = ((x * lax.rsqrt(ms + EPS)) * w_ref[...]).astype(o_ref.dtype)


def _rmsnorm(x, w, out_dtype, bm, name):
    m, d = x.shape
    return pl.pallas_call(
        _rmsnorm_kernel,
        grid=(m // bm,),
        in_specs=[pl.BlockSpec((bm, d), lambda i: (i, 0)), pl.BlockSpec((1, d), lambda i: (0, 0))],
        out_specs=pl.BlockSpec((bm, d), lambda i: (i, 0)),
        out_shape=jax.ShapeDtypeStruct((m, d), out_dtype),
        compiler_params=_params("parallel"),
        name=name,
    )(x, w.reshape(1, d))


def _matmul_kernel(a_ref, b_ref, o_ref):
    o_ref[...] = jnp.dot(a_ref[...], b_ref[...], preferred_element_type=F32).astype(o_ref.dtype)


def _matmul(a, b, out_dtype, bm, bn, name):
    m, k = a.shape
    n = b.shape[1]
    return pl.pallas_call(
        _matmul_kernel,
        grid=(n // bn, m // bm),
        in_specs=[pl.BlockSpec((bm, k), lambda j, i: (i, 0)), pl.BlockSpec((k, bn), lambda j, i: (0, j))],
        out_specs=pl.BlockSpec((bm, bn), lambda j, i: (i, j)),
        out_shape=jax.ShapeDtypeStruct((m, n), out_dtype),
        compiler_params=_params("parallel", "parallel"),
        name=name,
    )(a, b)


def _matmul_f32w_kernel(a_ref, w_ref, o_ref, w16_ref):
    @pl.when(pl.program_id(1) == 0)
    def _():
        w16_ref[...] = w_ref[...].astype(BF16)

    o_ref[...] = jnp.dot(a_ref[...], w16_ref[...], preferred_element_type=F32).astype(o_ref.dtype)


def _matmul_f32w(a, w, layer, n, out_dtype, bm, bn, name):
    m, k = a.shape
    assert n % bn == 0 and n <= w.shape[2]
    return pl.pallas_call(
        _matmul_f32w_kernel,
        grid=(n // bn, m // bm),
        in_specs=[pl.BlockSpec((bm, k), lambda j, i: (i, 0)), pl.BlockSpec((None, k, bn), lambda j, i: (layer, 0, j))],
        out_specs=pl.BlockSpec((bm, bn), lambda j, i: (i, j)),
        out_shape=jax.ShapeDtypeStruct((m, n), out_dtype),
        scratch_shapes=[pltpu.VMEM((k, bn), BF16)],
        compiler_params=_params("parallel", "arbitrary"),
        name=name,
    )(a, w)


def _dt_kernel(u_ref, w_ref, bias_ref, alog_ref, dt_ref, acs_ref, acst_ref):
    raw = jnp.dot(u_ref[...], w_ref[...], preferred_element_type=F32) + bias_ref[...]
    dt = jnp.maximum(raw, 0.0) + jnp.log1p(jnp.exp(-jnp.abs(raw)))
    a = dt * (-jnp.exp(alog_ref[...]))
    li = lax.broadcasted_iota(jnp.int32, (CHUNK, CHUNK), 0)
    ki = lax.broadcasted_iota(jnp.int32, (CHUNK, CHUNK), 1)
    tril = (ki <= li).astype(F32)
    acs = jnp.dot(tril, a, precision=lax.Precision.HIGHEST, preferred_element_type=F32)
    dt_ref[...] = dt
    acs_ref[...] = acs
    acst_ref[0] = acs.T


def _dt_prep(u, w_dt, dt_bias, a_log):
    m, d = u.shape
    h = w_dt.shape[1]
    nc = m // CHUNK
    return pl.pallas_call(
        _dt_kernel,
        grid=(nc,),
        in_specs=[
            pl.BlockSpec((CHUNK, d), lambda i: (i, 0)),
            pl.BlockSpec((d, h), lambda i: (0, 0)),
            pl.BlockSpec((1, h), lambda i: (0, 0)),
            pl.BlockSpec((1, h), lambda i: (0, 0)),
        ],
        out_specs=[
            pl.BlockSpec((CHUNK, h), lambda i: (i, 0)),
            pl.BlockSpec((CHUNK, h), lambda i: (i, 0)),
            pl.BlockSpec((1, h, CHUNK), lambda i: (i, 0, 0)),
        ],
        out_shape=[
            jax.ShapeDtypeStruct((m, h), F32),
            jax.ShapeDtypeStruct((m, h), F32),
            jax.ShapeDtypeStruct((nc, h, CHUNK), F32),
        ],
        compiler_params=_params("parallel"),
        name="dt_prep",
    )(u, w_dt, dt_bias.reshape(1, h), a_log.reshape(1, h))


def _split_expand(v, onehot2_ref):
    hi = v.astype(BF16)
    mid = (v - hi.astype(F32)).astype(BF16)
    return jnp.dot(jnp.concatenate([hi, mid], axis=1), onehot2_ref[...], preferred_element_type=F32)


def _ssd_kernel(xs_ref, b_ref, c_ref, z_ref, dt_ref, acs_ref, acst_ref, cw_ref, cbias_ref, shift_ref,
                rexp_ref, rexp2_ref, eexp2_ref, dskip_ref, nw_ref,
                y_ref, pad_ref, state_ref, ydiag_ref, *, r):
    gw = r * SSM_HEADDIM
    n = SSM_STATE
    halo = PACKED_ROWS

    @pl.when(pl.program_id(2) == 0)
    def _():
        pad_ref[0:halo, :] = jnp.zeros((halo, gw + 2 * n), BF16)
        state_ref[...] = jnp.zeros_like(state_ref)

    pad_ref[halo:halo + CHUNK, 0:gw] = xs_ref[...]
    pad_ref[halo:halo + CHUNK, gw:gw + n] = b_ref[...]
    pad_ref[halo:halo + CHUNK, gw + n:gw + 2 * n] = c_ref[...]
    shifted = jnp.dot(shift_ref[...], pad_ref[...], preferred_element_type=F32)
    conv = cbias_ref[0]
    for k in range(CONV_WIDTH - 1):
        conv = conv + shifted[k * CHUNK:(k + 1) * CHUNK] * cw_ref[0, k:k + 1, :]
    conv = conv + pad_ref[halo:halo + CHUNK, :].astype(F32) * cw_ref[0, CONV_WIDTH - 1:CONV_WIDTH, :]
    pad_ref[0:halo, :] = pad_ref[CHUNK:CHUNK + halo, :]
    xc = _silu(conv[:, 0:gw])
    bc = _silu(conv[:, gw:gw + n])
    cc = _silu(conv[:, gw + n:gw + 2 * n])

    cb16 = cc.astype(BF16)
    bb16 = bc.astype(BF16)
    cbm = lax.dot_general(cb16, bb16, (((1,), (1,)), ((), ())), preferred_element_type=F32)

    acs = acs_ref[...]
    acs_last = acs[CHUNK - 1:CHUNK, :]
    per_head = jnp.concatenate([dt_ref[...], jnp.exp(acs), jnp.exp(acs_last - acs)], axis=0)
    ex = jnp.dot(per_head.astype(BF16), rexp_ref[...], preferred_element_type=F32)
    dt_e = ex[0:CHUNK]
    eacs_e = ex[CHUNK:2 * CHUNK]
    dst_e = ex[2 * CHUNK:3 * CHUNK]
    last_e = _split_expand(jnp.broadcast_to(acs_last, (PACKED_ROWS, acs.shape[1])), rexp2_ref)
    cd_e = jnp.exp(last_e[0:1, :])
    col_all = _split_expand(acs, eexp2_ref)

    x32 = xc * dt_e
    x16 = x32.astype(BF16)
    li = lax.broadcasted_iota(jnp.int32, (CHUNK, CHUNK), 0)
    si = lax.broadcasted_iota(jnp.int32, (CHUNK, CHUNK), 1)
    causal = si <= li
    low_half = si < SSM_HEADDIM
    for p in range(r // 2):
        xp = x16[:, p * LANES:(p + 1) * LANES]
        ys = []
        for q in range(2):
            j = 2 * p + q
            seg = col_all[:, j * LANES:(j + 1) * LANES] - acst_ref[0, j:j + 1, :]
            mj = (jnp.exp(jnp.where(causal, seg, -jnp.inf)) * cbm).astype(BF16)
            ys.append(jnp.dot(mj, xp, preferred_element_type=F32))
        ydiag_ref[:, p * LANES:(p + 1) * LANES] = jnp.where(low_half, ys[0], ys[1])

    st = state_ref[...]
    y_off = jnp.dot(cb16, st.astype(BF16), preferred_element_type=F32) * eacs_e
    xd16 = (x32 * dst_e).astype(BF16)
    state_ref[...] = st * cd_e + jnp.dot(bc.T.astype(BF16), xd16, preferred_element_type=F32)

    y = ydiag_ref[...] + y_off + xc * dskip_ref[...]
    gv = y * _silu(z_ref[...].astype(F32))
    ms = jnp.mean(gv * gv, axis=-1, keepdims=True)
    y_ref[...] = ((gv * lax.rsqrt(ms + EPS)) * nw_ref[...]).astype(y_ref.dtype)


def _ssd(proj, off, dt, acs, acst, conv_w, conv_b, d_skip, ssm_norm_w, batch, seq, groups, d_ssm):
    m = proj.shape[0]
    heads = dt.shape[1]
    r = heads // groups
    gw = r * SSM_HEADDIM
    n = SSM_STATE
    nc = seq // CHUNK
    assert heads == LANES and r % 2 == 0 and gw % LANES == 0 and seq % CHUNK == 0

    head_of_col = jnp.arange(d_ssm, dtype=jnp.int32) // SSM_HEADDIM
    rexp = (jnp.arange(heads, dtype=jnp.int32)[:, None] == head_of_col[None, :]).astype(BF16)
    head_of_tile = jnp.arange(heads * LANES, dtype=jnp.int32) // LANES
    eexp = (jnp.arange(heads, dtype=jnp.int32)[:, None] == head_of_tile[None, :]).astype(BF16)
    rexp2 = jnp.concatenate([rexp] * 2, axis=0)
    eexp2 = jnp.concatenate([eexp] * 2, axis=0)
    dskip_e = jnp.repeat(d_skip.astype(F32), SSM_HEADDIM).reshape(1, d_ssm)

    def per_group(a):
        rows = a.shape[0]
        xs = a[:, :d_ssm].reshape(rows, groups, gw)
        bs = a[:, d_ssm:d_ssm + groups * n].reshape(rows, groups, n)
        cs = a[:, d_ssm + groups * n:].reshape(rows, groups, n)
        return jnp.concatenate([xs, bs, cs], axis=2).transpose(1, 0, 2)

    cw = per_group(conv_w.astype(F32))
    cb = per_group(conv_b.astype(F32).reshape(1, -1))
    halo = PACKED_ROWS
    t_idx = jnp.arange((CONV_WIDTH - 1) * CHUNK, dtype=jnp.int32)
    src = halo + (t_idx % CHUNK) - (CONV_WIDTH - 1) + t_idx // CHUNK
    shift = (src[:, None] == jnp.arange(halo + CHUNK, dtype=jnp.int32)[None, :]).astype(BF16)

    def row(b, g, c):
        return b * nc + c

    xs0, b0, c0, z0 = off["xs"] // gw, off["B"] // n, off["C"] // n, off["z"] // gw
    cols = gw + 2 * n
    in_specs = [
        pl.BlockSpec((CHUNK, gw), lambda b, g, c: (row(b, g, c), xs0 + g)),
        pl.BlockSpec((CHUNK, n), lambda b, g, c: (row(b, g, c), b0 + g)),
        pl.BlockSpec((CHUNK, n), lambda b, g, c: (row(b, g, c), c0 + g)),
        pl.BlockSpec((CHUNK, gw), lambda b, g, c: (row(b, g, c), z0 + g)),
        pl.BlockSpec((CHUNK, heads), lambda b, g, c: (row(b, g, c), 0)),
        pl.BlockSpec((CHUNK, heads), lambda b, g, c: (row(b, g, c), 0)),
        pl.BlockSpec((1, r, CHUNK), lambda b, g, c: (row(b, g, c), g, 0)),
        pl.BlockSpec((1, CONV_WIDTH, cols), lambda b, g, c: (g, 0, 0)),
        pl.BlockSpec((1, 1, cols), lambda b, g, c: (g, 0, 0)),
        pl.BlockSpec(((CONV_WIDTH - 1) * CHUNK, halo + CHUNK), lambda b, g, c: (0, 0)),
        pl.BlockSpec((heads, gw), lambda b, g, c: (0, g)),
        pl.BlockSpec((2 * heads, gw), lambda b, g, c: (0, g)),
        pl.BlockSpec((2 * heads, r * LANES), lambda b, g, c: (0, g)),
        pl.BlockSpec((1, gw), lambda b, g, c: (0, g)),
        pl.BlockSpec((1, gw), lambda b, g, c: (0, g)),
    ]
    return pl.pallas_call(
        functools.partial(_ssd_kernel, r=r),
        grid=(batch, groups, nc),
        in_specs=in_specs,
        out_specs=pl.BlockSpec((CHUNK, gw), lambda b, g, c: (row(b, g, c), g)),
        out_shape=jax.ShapeDtypeStruct((m, d_ssm), BF16),
        scratch_shapes=[
            pltpu.VMEM((halo + CHUNK, cols), BF16),
            pltpu.VMEM((n, gw), F32),
            pltpu.VMEM((CHUNK, gw), F32),
        ],
        compiler_params=_params("parallel", "parallel", "arbitrary"),
        name="ssd_scan",
    )(proj, proj, proj, proj, dt, acs, acst, cw, cb, shift,
      rexp, rexp2, eexp2, dskip_e, ssm_norm_w.astype(F32).reshape(1, d_ssm))


def _normed(x_refs, w_ref):
    x = jnp.concatenate([r[...] for r in x_refs], axis=1).astype(F32)
    ms = jnp.mean(x * x, axis=-1, keepdims=True)
    return ((x * lax.rsqrt(ms + EPS)) * w_ref[...]).astype(BF16)


def _piece_specs(offset, width, bm):
    piece = math.gcd(offset, width)
    assert piece % LANES == 0
    first = offset // piece
    return [pl.BlockSpec((bm, piece), lambda j, i, t=t: (i, first + t)) for t in range(width // piece)]


def _q_kernel(*refs, hb, scale):
    nw_ref, wq_ref, wsw_ref, cs1_ref, cs2_ref, o_ref = refs[-6:]
    cqn = _normed(refs[:-6], nw_ref)
    q = jnp.dot(cqn, wq_ref[...], preferred_element_type=F32)
    qs = jnp.dot(cqn, wsw_ref[...], preferred_element_type=F32)
    cs1 = cs1_ref[...]
    cs2 = cs2_ref[...]
    for h in range(hb):
        c0 = h * QK_PAD
        o_ref[:, c0:c0 + LANES] = (q[:, c0:c0 + LANES] * scale).astype(o_ref.dtype)
        roped = q[:, c0 + LANES:c0 + QK_PAD] * cs1 + qs[:, h * LANES:(h + 1) * LANES] * cs2
        o_ref[:, c0 + LANES:c0 + QK_PAD] = (roped * scale).astype(o_ref.dtype)


def _kv_kernel(*refs, hb):
    kr_ref, krsw_ref, nw_ref, w_ref, cs1_ref, cs2_ref, k_ref, v_ref = refs[-8:]
    ckv = _normed(refs[:-8], nw_ref)
    kv = jnp.dot(ckv, w_ref[...], preferred_element_type=F32)
    roped = (kr_ref[...].astype(F32) * cs1_ref[...] + krsw_ref[...].astype(F32) * cs2_ref[...]).astype(k_ref.dtype)
    lane = lax.broadcasted_iota(jnp.int32, roped.shape, 1)
    ones_col = jnp.where(lane == 0, 1.0, 0.0).astype(v_ref.dtype)
    for h in range(hb):
        c0 = h * QK_PAD
        k_ref[:, c0:c0 + LANES] = kv[:, c0:c0 + LANES].astype(k_ref.dtype)
        k_ref[:, c0 + LANES:c0 + QK_PAD] = roped
        v_ref[:, c0:c0 + V_HEAD] = kv[:, c0 + LANES:c0 + QK_PAD].astype(v_ref.dtype)
        v_ref[:, c0 + V_HEAD:c0 + V_PAD] = ones_col


def _rope_tables(seq):
    half = QK_ROPE // 2
    inv_freq = ROPE_THETA ** (-jnp.arange(0, half, dtype=F32) / half)
    ang = jnp.arange(seq, dtype=jnp.int32).astype(F32)[:, None] * inv_freq[None, :]
    cos, sin = jnp.cos(ang), jnp.sin(ang)
    zeros = jnp.zeros((seq, LANES - QK_ROPE), F32)
    return jnp.concatenate([cos, cos, zeros], axis=1), jnp.concatenate([-sin, sin, zeros], axis=1)


def _q_proj(proj, cq_off, q_norm_w, w_uq, cs1, cs2, seq, heads, bm, hb):
    m = proj.shape[0]
    ql = w_uq.shape[0]
    w = w_uq.reshape(ql, heads, QK_NOPE + QK_ROPE)
    nope, rp = w[:, :, :QK_NOPE], w[:, :, QK_NOPE:]
    half = QK_ROPE // 2
    zpad = jnp.zeros((ql, heads, LANES - QK_ROPE), w.dtype)
    wq = jnp.concatenate([nope, rp, zpad], axis=2).reshape(ql, heads * QK_PAD).astype(BF16)
    wsw = jnp.concatenate([rp[:, :, half:], rp[:, :, :half], zpad], axis=2).reshape(ql, heads * LANES).astype(BF16)
    nb = seq // bm
    scale = LOG2E / math.sqrt(QK_NOPE + QK_ROPE)
    cq_specs = _piece_specs(cq_off, ql, bm)
    return pl.pallas_call(
        functools.partial(_q_kernel, hb=hb, scale=scale),
        grid=(heads // hb, m // bm),
        in_specs=cq_specs + [
            pl.BlockSpec((1, ql), lambda j, i: (0, 0)),
            pl.BlockSpec((ql, hb * QK_PAD), lambda j, i: (0, j)),
            pl.BlockSpec((ql, hb * LANES), lambda j, i: (0, j)),
            pl.BlockSpec((bm, LANES), lambda j, i: (i % nb, 0)),
            pl.BlockSpec((bm, LANES), lambda j, i: (i % nb, 0)),
        ],
        out_specs=pl.BlockSpec((bm, hb * QK_PAD), lambda j, i: (i, j)),
        out_shape=jax.ShapeDtypeStruct((m, heads * QK_PAD), BF16),
        compiler_params=_params("parallel", "parallel"),
        name="q_proj",
    )(*([proj] * len(cq_specs)), q_norm_w.astype(F32).reshape(1, ql), wq, wsw, cs1, cs2)


def _kv_proj(proj, kvc_off, proj_tail, off, kv_norm_w, w_ukv, cs1, cs2, seq, heads, bm, hb):
    m = proj.shape[0]
    kvl = w_ukv.shape[0]
    nb = seq // bm
    kr0, krsw0 = off["kr"] // LANES, off["krsw"] // LANES
    kvc_specs = _piece_specs(kvc_off, kvl, bm)
    return pl.pallas_call(
        functools.partial(_kv_kernel, hb=hb),
        grid=(heads // hb, m // bm),
        in_specs=kvc_specs + [
            pl.BlockSpec((bm, LANES), lambda j, i: (i, kr0)),
            pl.BlockSpec((bm, LANES), lambda j, i: (i, krsw0)),
            pl.BlockSpec((1, kvl), lambda j, i: (0, 0)),
            pl.BlockSpec((kvl, hb * QK_PAD), lambda j, i: (0, j)),
            pl.BlockSpec((bm, LANES), lambda j, i: (i % nb, 0)),
            pl.BlockSpec((bm, LANES), lambda j, i: (i % nb, 0)),
        ],
        out_specs=[
            pl.BlockSpec((bm, hb * QK_PAD), lambda j, i: (i, j)),
            pl.BlockSpec((bm, hb * V_PAD), lambda j, i: (i, j)),
        ],
        out_shape=[
            jax.ShapeDtypeStruct((m, heads * QK_PAD), BF16),
            jax.ShapeDtypeStruct((m, heads * V_PAD), BF16),
        ],
        compiler_params=_params("parallel", "parallel"),
        name="kv_proj",
    )(*([proj] * len(kvc_specs)), proj_tail, proj_tail, kv_norm_w.astype(F32).reshape(1, kvl),
      w_ukv.astype(BF16), cs1, cs2)


def _attn_kernel(q_ref, k_ref, v_ref, g_ref, o_ref, m_ref, acc_ref, s_ref, *, blk, hpb):
    qi = pl.program_id(2)
    m_ref[...] = jnp.full(m_ref.shape, MASK_VALUE, F32)
    acc_ref[...] = jnp.zeros_like(acc_ref)

    def scores(j, slot):
        start = pl.multiple_of(j * blk, blk)
        for h in range(hpb):
            s_ref[slot, h] = lax.dot_general(
                q_ref[:, h * QK_PAD:(h + 1) * QK_PAD], k_ref[pl.ds(start, blk), h * QK_PAD:(h + 1) * QK_PAD],
                (((1,), (1,)), ((), ())), preferred_element_type=F32)

    def accumulate(j, slot, masked):
        start = pl.multiple_of(j * blk, blk)
        for h in range(hpb):
            s = s_ref[slot, h]
            v = v_ref[pl.ds(start, blk), h * V_PAD:(h + 1) * V_PAD]
            if masked:
                ri = lax.broadcasted_iota(jnp.int32, (blk, blk), 0)
                ci = lax.broadcasted_iota(jnp.int32, (blk, blk), 1)
                s = jnp.where(ci <= ri, s, MASK_VALUE)
            m_prev = m_ref[h]
            m_next = jnp.maximum(m_prev, jnp.max(s, axis=1, keepdims=True))
            p = jnp.exp2(s - jnp.concatenate([m_next] * (blk // LANES), axis=1))
            alpha = jnp.exp2(m_prev - m_next)
            pv = jnp.dot(p.astype(v.dtype), v, preferred_element_type=F32)
            acc_ref[h] = jnp.concatenate([alpha] * (V_PAD // LANES), axis=1) * acc_ref[h] + pv
            m_ref[h] = m_next

    scores(0, 0)

    def body(i, carry):
        j = 2 * i
        scores(j + 1, 1)
        accumulate(j, 0, False)
        scores(j + 2, 0)
        accumulate(j + 1, 1, False)
        return carry

    lax.fori_loop(0, qi // 2, body, 0)

    @pl.when(qi % 2 == 0)
    def _():
        accumulate(qi, 0, True)

    @pl.when(qi % 2 == 1)
    def _():
        scores(qi, 1)
        accumulate(qi - 1, 0, False)
        accumulate(qi, 1, True)

    for h in range(hpb):
        acc = acc_ref[h]
        g = g_ref[:, h * V_HEAD:(h + 1) * V_HEAD].astype(F32)
        o = acc[:, 0:V_HEAD] / acc[:, V_HEAD:V_HEAD + 1]
        o_ref[:, h * V_HEAD:(h + 1) * V_HEAD] = (o * _silu(g)).astype(o_ref.dtype)


def _attention(q, k, v, proj, off, batch, seq, heads, blk, hpb):
    m = q.shape[0]
    nq = seq // blk
    g0 = off["g_attn"] // (hpb * V_HEAD)
    return pl.pallas_call(
        functools.partial(_attn_kernel, blk=blk, hpb=hpb),
        grid=(batch, heads // hpb, nq),
        in_specs=[
            pl.BlockSpec((blk, hpb * QK_PAD), lambda b, h, i: (b * nq + i, h)),
            pl.BlockSpec((seq, hpb * QK_PAD), lambda b, h, i: (b, h)),
            pl.BlockSpec((seq, hpb * V_PAD), lambda b, h, i: (b, h)),
            pl.BlockSpec((blk, hpb * V_HEAD), lambda b, h, i: (b * nq + i, g0 + h)),
        ],
        out_specs=pl.BlockSpec((blk, hpb * V_HEAD), lambda b, h, i: (b * nq + i, h)),
        out_shape=jax.ShapeDtypeStruct((m, heads * V_HEAD), BF16),
        scratch_shapes=[
            pltpu.VMEM((hpb, blk, LANES), F32),
            pltpu.VMEM((hpb, blk, V_PAD), F32),
            pltpu.VMEM((2, hpb, blk, blk), F32),
        ],
        compiler_params=_params("parallel", "parallel", "arbitrary"),
        name="mla_attention",
    )(q, k, v, proj)


def _merge_kernel(ys_ref, ya_ref, ws_ref, wa_ref, gs_ref, ga_ref, o_ref):
    a = jnp.dot(ys_ref[...], ws_ref[...], preferred_element_type=F32)
    b = jnp.dot(ya_ref[...], wa_ref[...], preferred_element_type=F32)
    o = _sigmoid(gs_ref[...].astype(F32)) * a + _sigmoid(ga_ref[...].astype(F32)) * b
    o_ref[...] = o.astype(o_ref.dtype)


def _merge(y_ssm, y_attn, w_s, w_a, proj, off, bm, bn):
    m, ds = y_ssm.shape
    da = y_attn.shape[1]
    d = w_s.shape[1]
    gs0, ga0 = off["gate_ssm"] // bn, off["gate_attn"] // bn
    return pl.pallas_call(
        _merge_kernel,
        grid=(d // bn, m // bm),
        in_specs=[
            pl.BlockSpec((bm, ds), lambda j, i: (i, 0)),
            pl.BlockSpec((bm, da), lambda j, i: (i, 0)),
            pl.BlockSpec((ds, bn), lambda j, i: (0, j)),
            pl.BlockSpec((da, bn), lambda j, i: (0, j)),
            pl.BlockSpec((bm, bn), lambda j, i: (i, gs0 + j)),
            pl.BlockSpec((bm, bn), lambda j, i: (i, ga0 + j)),
        ],
        out_specs=pl.BlockSpec((bm, bn), lambda j, i: (i, j)),
        out_shape=jax.ShapeDtypeStruct((m, d), BF16),
        compiler_params=_params("parallel", "parallel"),
        name="merge_proj",
    )(y_ssm, y_attn, w_s, w_a, proj, proj)


def _out_kernel(a_ref, b_ref, x_ref, nw_ref, o_ref, *, nj, bn, final_norm):
    j = pl.program_id(1)
    hblk = x_ref[...] + jnp.dot(a_ref[...], b_ref[...], preferred_element_type=F32)
    for jj in range(nj):
        @pl.when(j == jj)
        def _(jj=jj):
            o_ref[:, jj * bn:(jj + 1) * bn] = hblk

    if final_norm:
        @pl.when(j == nj - 1)
        def _():
            hf = o_ref[...]
            ms = jnp.mean(hf * hf, axis=-1, keepdims=True)
            o_ref[...] = (hf * lax.rsqrt(ms + EPS)) * nw_ref[...]


def _out_proj(merged, w_out, x, norm_w, final_norm, bm, bn):
    m, k = merged.shape
    n = w_out.shape[1]
    nj = n // bn
    return pl.pallas_call(
        functools.partial(_out_kernel, nj=nj, bn=bn, final_norm=final_norm),
        grid=(m // bm, nj),
        in_specs=[
            pl.BlockSpec((bm, k), lambda i, j: (i, 0)),
            pl.BlockSpec((k, bn), lambda i, j: (0, j)),
            pl.BlockSpec((bm, bn), lambda i, j: (i, j)),
            pl.BlockSpec((1, n), lambda i, j: (0, 0)),
        ],
        out_specs=pl.BlockSpec((bm, n), lambda i, j: (i, 0)),
        out_shape=jax.ShapeDtypeStruct((m, n), F32),
        compiler_params=_params("parallel", "arbitrary"),
        name="out_proj",
    )(merged, w_out, x, norm_w.astype(F32).reshape(1, n))


def _in_proj_layout(w_in, d_ssm, groups, heads_ssm, q_lora, kv_lora, d_attn, d_model, attn_align, bn_merge, bn_in):
    gn = groups * SSM_STATE
    sizes = (d_ssm, d_ssm + 2 * gn, heads_ssm, q_lora, kv_lora + QK_ROPE, d_attn, d_model, d_model)
    starts = [0]
    for s in sizes[:-1]:
        starts.append(starts[-1] + s)
    z0, xbc0, dt0, cq0, kv0, ga0, gs0, gt0 = starts
    head_off = {"z": z0, "xs": xbc0, "B": xbc0 + d_ssm, "C": xbc0 + d_ssm + gn, "cq": cq0, "kvc": kv0}
    n_head = -(-(kv0 + kv_lora) // bn_in) * bn_in

    half = QK_ROPE // 2
    d = w_in.shape[0]
    w_kr = w_in[:, kv0 + kv_lora:kv0 + kv_lora + QK_ROPE]
    lane_pad = jnp.zeros((d, LANES - QK_ROPE), w_in.dtype)
    pieces = [
        ("kr", jnp.concatenate([w_kr, lane_pad], axis=1), LANES),
        ("krsw", jnp.concatenate([w_kr[:, half:], w_kr[:, :half], lane_pad], axis=1), LANES),
        ("g_attn", w_in[:, ga0:ga0 + d_attn], attn_align),
        ("gate_ssm", w_in[:, gs0:gs0 + d_model], bn_merge),
        ("gate_attn", w_in[:, gt0:gt0 + d_model], bn_merge),
    ]
    tail_off = {}
    cols = []
    pos = 0
    for name, w, align in pieces:
        start = -(-pos // align) * align
        if start > pos:
            cols.append(jnp.zeros((d, start - pos), BF16))
        tail_off[name] = start
        cols.append(w.astype(BF16))
        pos = start + w.shape[1]
    total = -(-pos // bn_in) * bn_in
    if total > pos:
        cols.append(jnp.zeros((d, total - pos), BF16))
    return n_head, head_off, jnp.concatenate(cols, axis=1), tail_off, w_in[:, dt0:dt0 + heads_ssm].astype(BF16)


def kernel(x, norm_in_w, w_in, conv_w, conv_b, dt_bias, a_log, d_skip, ssm_norm_w, q_norm_w, w_uq,
           kv_norm_w, w_ukv, w_branch_ssm, w_branch_attn, w_out, norm_final_w):
    batch, seq, d_model = x.shape
    depth = w_in.shape[0]
    m = batch * seq
    d_ssm = w_branch_ssm.shape[1]
    d_attn = w_branch_attn.shape[1]
    heads_ssm = dt_bias.shape[1]
    groups = (conv_w.shape[2] - d_ssm) // (2 * SSM_STATE)
    gw = (heads_ssm // groups) * SSM_HEADDIM
    q_lora, kv_lora = w_uq.shape[1], w_ukv.shape[1]
    heads = d_attn // V_HEAD
    assert heads_ssm * SSM_HEADDIM == d_ssm and w_uq.shape[2] == heads * (QK_NOPE + QK_ROPE)

    bm_norm = min(256, m)
    bm_in, bn_in = min(1024, m), 512
    bm_qkv, hb_qkv = min(512, seq), 8
    blk_attn, hpb_attn = min(512, seq), 2
    bm_merge, bn_merge = min(512, m), 512
    bm_out, bn_out = min(512, m), 512

    cs1, cs2 = _rope_tables(seq)
    h = x.reshape(m, d_model)
    for layer in range(depth):
        n_head, off, w_tail, toff, w_dt = _in_proj_layout(
            w_in[layer], d_ssm, groups, heads_ssm, q_lora, kv_lora, d_attn, d_model,
            hpb_attn * V_HEAD, bn_merge, bn_in)
        u = _rmsnorm(h, norm_in_w[layer], BF16, bm_norm, "norm_in")
        proj = _matmul_f32w(u, w_in, layer, n_head, BF16, bm_in, bn_in, "in_proj_head")
        proj_tail = _matmul(u, w_tail, BF16, bm_in, bn_in, "in_proj_tail")

        dt, acs, acst = _dt_prep(u, w_dt, dt_bias[layer].astype(F32), a_log[layer].astype(F32))
        y_ssm = _ssd(proj, off, dt, acs, acst, conv_w[layer], conv_b[layer], d_skip[layer],
                     ssm_norm_w[layer], batch, seq, groups, d_ssm)

        q = _q_proj(proj, off["cq"], q_norm_w[layer], w_uq[layer], cs1, cs2, seq, heads, bm_qkv, hb_qkv)
        k, v = _kv_proj(proj, off["kvc"], proj_tail, toff, kv_norm_w[layer], w_ukv[layer], cs1, cs2, seq, heads,
                        bm_qkv, hb_qkv)
        y_attn = _attention(q, k, v, proj_tail, toff, batch, seq, heads, blk_attn, hpb_attn)

        merged = _merge(y_ssm, y_attn, w_branch_ssm[layer].astype(BF16), w_branch_attn[layer].astype(BF16),
                        proj_tail, toff, bm_merge, bn_merge)
        last = layer == depth - 1
        h = _out_proj(merged, w_out[layer].astype(BF16), h, norm_final_w, last, bm_out, bn_out)
    return h.reshape(batch, seq, d_model)
```

```python
import functools
import math

import jax
import jax.numpy as jnp
from jax import lax
from jax.experimental import pallas as pl
from jax.experimental.pallas import tpu as pltpu

F32 = jnp.float32
BF16 = jnp.bfloat16

EPS = 1e-6
SSM_HEADDIM = 64
SSM_STATE = 128
CONV_WIDTH = 4
CHUNK = 128
QK_NOPE = 128
QK_ROPE = 64
V_HEAD = 128
ROPE_THETA = 10000.0

LANES = 128
SUBLANES = 8
PACKED_ROWS = 16
VMEM_LIMIT_BYTES = 56 * 1024 * 1024
QK_PAD = 2 * LANES
V_PAD = 2 * LANES
MASK_VALUE = -1e30
LOG2E = math.log2(math.e)


def _params(*sem):
    return pltpu.CompilerParams(dimension_semantics=sem, vmem_limit_bytes=VMEM_LIMIT_BYTES)


def _sigmoid(x):
    return 0.5 + 0.5 * jnp.tanh(0.5 * x)


def _silu(x):
    h = 0.5 * x
    return h + h * jnp.tanh(h)


def _rmsnorm_kernel(x_ref, w_ref, o_ref):
    x = x_ref[...].astype(F32)
    ms = jnp.mean(x * x, axis=-1, keepdims=True)
    o_ref[...] = ((x * lax.rsqrt(ms + EPS)) * w_ref[...]).astype(o_ref.dtype)


def _rmsnorm(x, w, out_dtype, bm, name):
    m, d = x.shape
    return pl.pallas_call(
        _rmsnorm_kernel,
        grid=(m // bm,),
        in_specs=[pl.BlockSpec((bm, d), lambda i: (i, 0)), pl.BlockSpec((1, d), lambda i: (0, 0))],
        out_specs=pl.BlockSpec((bm, d), lambda i: (i, 0)),
        out_shape=jax.ShapeDtypeStruct((m, d), out_dtype),
        compiler_params=_params("parallel"),
        name=name,
    )(x, w.reshape(1, d))


def _matmul_kernel(a_ref, b_ref, o_ref):
    o_ref[...] = jnp.dot(a_ref[...], b_ref[...], preferred_element_type=F32).astype(o_ref.dtype)


def _matmul(a, b, out_dtype, bm, bn, name):
    m, k = a.shape
    n = b.shape[1]
    return pl.pallas_call(
        _matmul_kernel,
        grid=(n // bn, m // bm),
        in_specs=[pl.BlockSpec((bm, k), lambda j, i: (i, 0)), pl.BlockSpec((k, bn), lambda j, i: (0, j))],
        out_specs=pl.BlockSpec((bm, bn), lambda j, i: (i, j)),
        out_shape=jax.ShapeDtypeStruct((m, n), out_dtype),
        compiler_params=_params("parallel", "parallel"),
        name=name,
    )(a, b)


_NT = (((1,), (1,)), ((), ()))


def _matmul_wt_kernel(*refs, npiece, piece):
    a_ref, w_refs, o_ref, w16_ref = refs[1], refs[2:2 + npiece], refs[2 + npiece], refs[3 + npiece]

    @pl.when(pl.program_id(1) == 0)
    def _():
        for t, w_ref in enumerate(w_refs):
            w16_ref[t * piece:(t + 1) * piece, :] = w_ref[...].astype(BF16)

    o_ref[...] = lax.dot_general(a_ref[...], w16_ref[...], _NT, preferred_element_type=F32).astype(o_ref.dtype)


def _matmul_wt(a, wt, layer, tile_rows, piece, out_dtype, bm, bn, name):
    m, k = a.shape
    npiece = bn // piece
    n_tiles = len(tile_rows)
    starts = jnp.asarray(tile_rows, jnp.int32) // piece
    assert all(r % piece == 0 for r in tile_rows)

    def piece_spec(t):
        return pl.BlockSpec((None, piece, k), lambda j, i, s: (layer, s[j] + t, 0))

    return pl.pallas_call(
        functools.partial(_matmul_wt_kernel, npiece=npiece, piece=piece),
        grid_spec=pltpu.PrefetchScalarGridSpec(
            num_scalar_prefetch=1,
            grid=(n_tiles, m // bm),
            in_specs=[pl.BlockSpec((bm, k), lambda j, i, s: (i, 0))] + [piece_spec(t) for t in range(npiece)],
            out_specs=pl.BlockSpec((bm, bn), lambda j, i, s: (i, j)),
            scratch_shapes=[pltpu.VMEM((bn, k), BF16)],
        ),
        out_shape=jax.ShapeDtypeStruct((m, n_tiles * bn), out_dtype),
        compiler_params=_params("parallel", "arbitrary"),
        name=name,
    )(starts, a, *([wt] * npiece))


def _dt_kernel(u_ref, wt_ref, bias_ref, alog_ref, dt_ref, acs_ref, acst_ref):
    raw = lax.dot_general(u_ref[...], wt_ref[...].astype(BF16), _NT, preferred_element_type=F32) + bias_ref[...]
    dt = jnp.maximum(raw, 0.0) + jnp.log1p(jnp.exp(-jnp.abs(raw)))
    a = dt * (-jnp.exp(alog_ref[...]))
    li = lax.broadcasted_iota(jnp.int32, (CHUNK, CHUNK), 0)
    ki = lax.broadcasted_iota(jnp.int32, (CHUNK, CHUNK), 1)
    tril = (ki <= li).astype(F32)
    acs = jnp.dot(tril, a, precision=lax.Precision.HIGHEST, preferred_element_type=F32)
    dt_ref[...] = dt
    acs_ref[...] = acs
    acst_ref[0] = acs.T


def _dt_prep(u, wt, layer, dt_row0, dt_bias, a_log):
    m, d = u.shape
    h = dt_bias.shape[0]
    nc = m // CHUNK
    assert dt_row0 % h == 0
    return pl.pallas_call(
        _dt_kernel,
        grid=(nc,),
        in_specs=[
            pl.BlockSpec((CHUNK, d), lambda i: (i, 0)),
            pl.BlockSpec((None, h, d), lambda i: (layer, dt_row0 // h, 0)),
            pl.BlockSpec((1, h), lambda i: (0, 0)),
            pl.BlockSpec((1, h), lambda i: (0, 0)),
        ],
        out_specs=[
            pl.BlockSpec((CHUNK, h), lambda i: (i, 0)),
            pl.BlockSpec((CHUNK, h), lambda i: (i, 0)),
            pl.BlockSpec((1, h, CHUNK), lambda i: (i, 0, 0)),
        ],
        out_shape=[
            jax.ShapeDtypeStruct((m, h), F32),
            jax.ShapeDtypeStruct((m, h), F32),
            jax.ShapeDtypeStruct((nc, h, CHUNK), F32),
        ],
        compiler_params=_params("parallel"),
        name="dt_prep",
    )(u, wt, dt_bias.reshape(1, h), a_log.reshape(1, h))


def _kr_kernel(u_ref, wt_ref, o_ref):
    w = wt_ref[...]
    half = QK_ROPE // 2
    zeros = jnp.zeros((LANES - QK_ROPE, w.shape[1]), w.dtype)
    w_all = jnp.concatenate([w, zeros, w[half:], w[:half], zeros], axis=0).astype(BF16)
    o_ref[...] = lax.dot_general(u_ref[...], w_all, _NT, preferred_element_type=F32).astype(o_ref.dtype)


def _kr_proj(u, wt, layer, kr_row0, bm):
    m, d = u.shape
    assert kr_row0 % QK_ROPE == 0
    return pl.pallas_call(
        _kr_kernel,
        grid=(m // bm,),
        in_specs=[
            pl.BlockSpec((bm, d), lambda i: (i, 0)),
            pl.BlockSpec((None, QK_ROPE, d), lambda i: (layer, kr_row0 // QK_ROPE, 0)),
        ],
        out_specs=pl.BlockSpec((bm, 2 * LANES), lambda i: (i, 0)),
        out_shape=jax.ShapeDtypeStruct((m, 2 * LANES), BF16),
        compiler_params=_params("parallel"),
        name="kr_proj",
    )(u, wt)


def _split_expand(v, onehot2_ref):
    hi = v.astype(BF16)
    mid = (v - hi.astype(F32)).astype(BF16)
    return jnp.dot(jnp.concatenate([hi, mid], axis=1), onehot2_ref[...], preferred_element_type=F32)


def _ssd_kernel(xs_ref, b_ref, c_ref, z_ref, dt_ref, acs_ref, acst_ref, cw_ref, cbias_ref, shift_ref,
                rexp_ref, rexp2_ref, eexp2_ref, dskip_ref, nw_ref,
                y_ref, pad_ref, state_ref, ydiag_ref, *, r):
    gw = r * SSM_HEADDIM
    n = SSM_STATE
    halo = PACKED_ROWS

    @pl.when(pl.program_id(2) == 0)
    def _():
        pad_ref[0:halo, :] = jnp.zeros((halo, gw + 2 * n), BF16)
        state_ref[...] = jnp.zeros_like(state_ref)

    pad_ref[halo:halo + CHUNK, 0:gw] = xs_ref[...]
    pad_ref[halo:halo + CHUNK, gw:gw + n] = b_ref[...]
    pad_ref[halo:halo + CHUNK, gw + n:gw + 2 * n] = c_ref[...]
    shifted = jnp.dot(shift_ref[...], pad_ref[...], preferred_element_type=F32)
    conv = cbias_ref[0]
    for k in range(CONV_WIDTH - 1):
        conv = conv + shifted[k * CHUNK:(k + 1) * CHUNK] * cw_ref[0, k:k + 1, :]
    conv = conv + pad_ref[halo:halo + CHUNK, :].astype(F32) * cw_ref[0, CONV_WIDTH - 1:CONV_WIDTH, :]
    pad_ref[0:halo, :] = pad_ref[CHUNK:CHUNK + halo, :]
    xc = _silu(conv[:, 0:gw])
    bc = _silu(conv[:, gw:gw + n])
    cc = _silu(conv[:, gw + n:gw + 2 * n])

    cb16 = cc.astype(BF16)
    bb16 = bc.astype(BF16)
    cbm = lax.dot_general(cb16, bb16, (((1,), (1,)), ((), ())), preferred_element_type=F32)

    acs = acs_ref[...]
    acs_last = acs[CHUNK - 1:CHUNK, :]
    per_head = jnp.concatenate([dt_ref[...], jnp.exp(acs), jnp.exp(acs_last - acs)], axis=0)
    ex = jnp.dot(per_head.astype(BF16), rexp_ref[...], preferred_element_type=F32)
    dt_e = ex[0:CHUNK]
    eacs_e = ex[CHUNK:2 * CHUNK]
    dst_e = ex[2 * CHUNK:3 * CHUNK]
    last_e = _split_expand(jnp.broadcast_to(acs_last, (PACKED_ROWS, acs.shape[1])), rexp2_ref)
    cd_e = jnp.exp(last_e[0:1, :])
    col_all = _split_expand(acs, eexp2_ref)

    x32 = xc * dt_e
    x16 = x32.astype(BF16)
    li = lax.broadcasted_iota(jnp.int32, (CHUNK, CHUNK), 0)
    si = lax.broadcasted_iota(jnp.int32, (CHUNK, CHUNK), 1)
    causal = si <= li
    low_half = si < SSM_HEADDIM
    for p in range(r // 2):
        xp = x16[:, p * LANES:(p + 1) * LANES]
        ys = []
        for q in range(2):
            j = 2 * p + q
            seg = col_all[:, j * LANES:(j + 1) * LANES] - acst_ref[0, j:j + 1, :]
            mj = (jnp.exp(jnp.where(causal, seg, -jnp.inf)) * cbm).astype(BF16)
            ys.append(jnp.dot(mj, xp, preferred_element_type=F32))
        ydiag_ref[:, p * LANES:(p + 1) * LANES] = jnp.where(low_half, ys[0], ys[1])

    st = state_ref[...]
    y_off = jnp.dot(cb16, st.astype(BF16), preferred_element_type=F32) * eacs_e
    xd16 = (x32 * dst_e).astype(BF16)
    state_ref[...] = st * cd_e + jnp.dot(bc.T.astype(BF16), xd16, preferred_element_type=F32)

    y = ydiag_ref[...] + y_off + xc * dskip_ref[...]
    gv = y * _silu(z_ref[...].astype(F32))
    ms = jnp.mean(gv * gv, axis=-1, keepdims=True)
    y_ref[...] = ((gv * lax.rsqrt(ms + EPS)) * nw_ref[...]).astype(y_ref.dtype)


def _ssd(proj, off, dt, acs, acst, conv_w, conv_b, d_skip, ssm_norm_w, batch, seq, groups, d_ssm):
    m = proj.shape[0]
    heads = dt.shape[1]
    r = heads // groups
    gw = r * SSM_HEADDIM
    n = SSM_STATE
    nc = seq // CHUNK
    assert heads == LANES and r % 2 == 0 and gw % LANES == 0 and seq % CHUNK == 0

    head_of_col = jnp.arange(d_ssm, dtype=jnp.int32) // SSM_HEADDIM
    rexp = (jnp.arange(heads, dtype=jnp.int32)[:, None] == head_of_col[None, :]).astype(BF16)
    head_of_tile = jnp.arange(heads * LANES, dtype=jnp.int32) // LANES
    eexp = (jnp.arange(heads, dtype=jnp.int32)[:, None] == head_of_tile[None, :]).astype(BF16)
    rexp2 = jnp.concatenate([rexp] * 2, axis=0)
    eexp2 = jnp.concatenate([eexp] * 2, axis=0)
    dskip_e = jnp.repeat(d_skip.astype(F32), SSM_HEADDIM).reshape(1, d_ssm)

    def per_group(a):
        rows = a.shape[0]
        xs = a[:, :d_ssm].reshape(rows, groups, gw)
        bs = a[:, d_ssm:d_ssm + groups * n].reshape(rows, groups, n)
        cs = a[:, d_ssm + groups * n:].reshape(rows, groups, n)
        return jnp.concatenate([xs, bs, cs], axis=2).transpose(1, 0, 2)

    cw = per_group(conv_w.astype(F32))
    cb = per_group(conv_b.astype(F32).reshape(1, -1))
    halo = PACKED_ROWS
    t_idx = jnp.arange((CONV_WIDTH - 1) * CHUNK, dtype=jnp.int32)
    src = halo + (t_idx % CHUNK) - (CONV_WIDTH - 1) + t_idx // CHUNK
    shift = (src[:, None] == jnp.arange(halo + CHUNK, dtype=jnp.int32)[None, :]).astype(BF16)

    def row(b, g, c):
        return b * nc + c

    xs0, b0, c0, z0 = off["xs"] // gw, off["B"] // n, off["C"] // n, off["z"] // gw
    cols = gw + 2 * n
    in_specs = [
        pl.BlockSpec((CHUNK, gw), lambda b, g, c: (row(b, g, c), xs0 + g)),
        pl.BlockSpec((CHUNK, n), lambda b, g, c: (row(b, g, c), b0 + g)),
        pl.BlockSpec((CHUNK, n), lambda b, g, c: (row(b, g, c), c0 + g)),
        pl.BlockSpec((CHUNK, gw), lambda b, g, c: (row(b, g, c), z0 + g)),
        pl.BlockSpec((CHUNK, heads), lambda b, g, c: (row(b, g, c), 0)),
        pl.BlockSpec((CHUNK, heads), lambda b, g, c: (row(b, g, c), 0)),
        pl.BlockSpec((1, r, CHUNK), lambda b, g, c: (row(b, g, c), g, 0)),
        pl.BlockSpec((1, CONV_WIDTH, cols), lambda b, g, c: (g, 0, 0)),
        pl.BlockSpec((1, 1, cols), lambda b, g, c: (g, 0, 0)),
        pl.BlockSpec(((CONV_WIDTH - 1) * CHUNK, halo + CHUNK), lambda b, g, c: (0, 0)),
        pl.BlockSpec((heads, gw), lambda b, g, c: (0, g)),
        pl.BlockSpec((2 * heads, gw), lambda b, g, c: (0, g)),
        pl.BlockSpec((2 * heads, r * LANES), lambda b, g, c: (0, g)),
        pl.BlockSpec((1, gw), lambda b, g, c: (0, g)),
        pl.BlockSpec((1, gw), lambda b, g, c: (0, g)),
    ]
    return pl.pallas_call(
        functools.partial(_ssd_kernel, r=r),
        grid=(batch, groups, nc),
        in_specs=in_specs,
        out_specs=pl.BlockSpec((CHUNK, gw), lambda b, g, c: (row(b, g, c), g)),
        out_shape=jax.ShapeDtypeStruct((m, d_ssm), BF16),
        scratch_shapes=[
            pltpu.VMEM((halo + CHUNK, cols), BF16),
            pltpu.VMEM((n, gw), F32),
            pltpu.VMEM((CHUNK, gw), F32),
        ],
        compiler_params=_params("parallel", "parallel", "arbitrary"),
        name="ssd_scan",
    )(proj, proj, proj, proj, dt, acs, acst, cw, cb, shift,
      rexp, rexp2, eexp2, dskip_e, ssm_norm_w.astype(F32).reshape(1, d_ssm))


def _normed(x_refs, w_ref):
    x = jnp.concatenate([r[...] for r in x_refs], axis=1).astype(F32)
    ms = jnp.mean(x * x, axis=-1, keepdims=True)
    return ((x * lax.rsqrt(ms + EPS)) * w_ref[...]).astype(BF16)


def _piece_specs(offset, width, bm):
    piece = math.gcd(offset, width)
    assert piece % LANES == 0
    first = offset // piece
    return [pl.BlockSpec((bm, piece), lambda j, i, t=t: (i, first + t)) for t in range(width // piece)]


def _q_kernel(*refs, hb, scale):
    nw_ref, wq_ref, wsw_ref, cs1_ref, cs2_ref, o_ref = refs[-6:]
    cqn = _normed(refs[:-6], nw_ref)
    q = jnp.dot(cqn, wq_ref[...], preferred_element_type=F32)
    qs = jnp.dot(cqn, wsw_ref[...], preferred_element_type=F32)
    cs1 = cs1_ref[...]
    cs2 = cs2_ref[...]
    for h in range(hb):
        c0 = h * QK_PAD
        o_ref[:, c0:c0 + LANES] = (q[:, c0:c0 + LANES] * scale).astype(o_ref.dtype)
        roped = q[:, c0 + LANES:c0 + QK_PAD] * cs1 + qs[:, h * LANES:(h + 1) * LANES] * cs2
        o_ref[:, c0 + LANES:c0 + QK_PAD] = (roped * scale).astype(o_ref.dtype)


def _kv_kernel(*refs, hb):
    kr_ref, krsw_ref, nw_ref, w_ref, cs1_ref, cs2_ref, k_ref, v_ref = refs[-8:]
    ckv = _normed(refs[:-8], nw_ref)
    kv = jnp.dot(ckv, w_ref[...], preferred_element_type=F32)
    roped = (kr_ref[...].astype(F32) * cs1_ref[...] + krsw_ref[...].astype(F32) * cs2_ref[...]).astype(k_ref.dtype)
    lane = lax.broadcasted_iota(jnp.int32, roped.shape, 1)
    ones_col = jnp.where(lane == 0, 1.0, 0.0).astype(v_ref.dtype)
    for h in range(hb):
        c0 = h * QK_PAD
        k_ref[:, c0:c0 + LANES] = kv[:, c0:c0 + LANES].astype(k_ref.dtype)
        k_ref[:, c0 + LANES:c0 + QK_PAD] = roped
        v_ref[:, c0:c0 + V_HEAD] = kv[:, c0 + LANES:c0 + QK_PAD].astype(v_ref.dtype)
        v_ref[:, c0 + V_HEAD:c0 + V_PAD] = ones_col


def _rope_tables(seq):
    half = QK_ROPE // 2
    inv_freq = ROPE_THETA ** (-jnp.arange(0, half, dtype=F32) / half)
    ang = jnp.arange(seq, dtype=jnp.int32).astype(F32)[:, None] * inv_freq[None, :]
    cos, sin = jnp.cos(ang), jnp.sin(ang)
    zeros = jnp.zeros((seq, LANES - QK_ROPE), F32)
    return jnp.concatenate([cos, cos, zeros], axis=1), jnp.concatenate([-sin, sin, zeros], axis=1)


def _q_proj(proj, cq_off, q_norm_w, w_uq, cs1, cs2, seq, heads, bm, hb):
    m = proj.shape[0]
    ql = w_uq.shape[0]
    w = w_uq.reshape(ql, heads, QK_NOPE + QK_ROPE)
    nope, rp = w[:, :, :QK_NOPE], w[:, :, QK_NOPE:]
    half = QK_ROPE // 2
    zpad = jnp.zeros((ql, heads, LANES - QK_ROPE), w.dtype)
    wq = jnp.concatenate([nope, rp, zpad], axis=2).reshape(ql, heads * QK_PAD).astype(BF16)
    wsw = jnp.concatenate([rp[:, :, half:], rp[:, :, :half], zpad], axis=2).reshape(ql, heads * LANES).astype(BF16)
    nb = seq // bm
    scale = LOG2E / math.sqrt(QK_NOPE + QK_ROPE)
    cq_specs = _piece_specs(cq_off, ql, bm)
    return pl.pallas_call(
        functools.partial(_q_kernel, hb=hb, scale=scale),
        grid=(heads // hb, m // bm),
        in_specs=cq_specs + [
            pl.BlockSpec((1, ql), lambda j, i: (0, 0)),
            pl.BlockSpec((ql, hb * QK_PAD), lambda j, i: (0, j)),
            pl.BlockSpec((ql, hb * LANES), lambda j, i: (0, j)),
            pl.BlockSpec((bm, LANES), lambda j, i: (i % nb, 0)),
            pl.BlockSpec((bm, LANES), lambda j, i: (i % nb, 0)),
        ],
        out_specs=pl.BlockSpec((bm, hb * QK_PAD), lambda j, i: (i, j)),
        out_shape=jax.ShapeDtypeStruct((m, heads * QK_PAD), BF16),
        compiler_params=_params("parallel", "parallel"),
        name="q_proj",
    )(*([proj] * len(cq_specs)), q_norm_w.astype(F32).reshape(1, ql), wq, wsw, cs1, cs2)


def _kv_proj(proj, kvc_off, kr, kv_norm_w, w_ukv, cs1, cs2, seq, heads, bm, hb):
    m = proj.shape[0]
    kvl = w_ukv.shape[0]
    nb = seq // bm
    kr0, krsw0 = 0, 1
    kvc_specs = _piece_specs(kvc_off, kvl, bm)
    return pl.pallas_call(
        functools.partial(_kv_kernel, hb=hb),
        grid=(heads // hb, m // bm),
        in_specs=kvc_specs + [
            pl.BlockSpec((bm, LANES), lambda j, i: (i, kr0)),
            pl.BlockSpec((bm, LANES), lambda j, i: (i, krsw0)),
            pl.BlockSpec((1, kvl), lambda j, i: (0, 0)),
            pl.BlockSpec((kvl, hb * QK_PAD), lambda j, i: (0, j)),
            pl.BlockSpec((bm, LANES), lambda j, i: (i % nb, 0)),
            pl.BlockSpec((bm, LANES), lambda j, i: (i % nb, 0)),
        ],
        out_specs=[
            pl.BlockSpec((bm, hb * QK_PAD), lambda j, i: (i, j)),
            pl.BlockSpec((bm, hb * V_PAD), lambda j, i: (i, j)),
        ],
        out_shape=[
            jax.ShapeDtypeStruct((m, heads * QK_PAD), BF16),
            jax.ShapeDtypeStruct((m, heads * V_PAD), BF16),
        ],
        compiler_params=_params("parallel", "parallel"),
        name="kv_proj",
    )(*([proj] * len(kvc_specs)), kr, kr, kv_norm_w.astype(F32).reshape(1, kvl),
      w_ukv.astype(BF16), cs1, cs2)


def _attn_kernel(q_ref, k_ref, v_ref, g_ref, o_ref, m_ref, acc_ref, s_ref, *, bq, bk, hpb):
    qi = pl.program_id(2)
    m_ref[...] = jnp.full(m_ref.shape, MASK_VALUE, F32)
    acc_ref[...] = jnp.zeros_like(acc_ref)

    def scores(j, slot, r0):
        start = pl.multiple_of(j * bk, bk)
        for h in range(hpb):
            s_ref[slot, h, r0:bq, :] = lax.dot_general(
                q_ref[r0:bq, h * QK_PAD:(h + 1) * QK_PAD], k_ref[pl.ds(start, bk), h * QK_PAD:(h + 1) * QK_PAD],
                _NT, preferred_element_type=F32)

    def accumulate(j, slot, r0, masked):
        start = pl.multiple_of(j * bk, bk)
        rows = bq - r0
        for h in range(hpb):
            s = s_ref[slot, h, r0:bq, :]
            v = v_ref[pl.ds(start, bk), h * V_PAD:(h + 1) * V_PAD]
            if masked:
                ri = lax.broadcasted_iota(jnp.int32, (rows, bk), 0)
                ci = lax.broadcasted_iota(jnp.int32, (rows, bk), 1)
                s = jnp.where(ci <= ri, s, MASK_VALUE)
            m_prev = m_ref[h, r0:bq, :]
            m_next = jnp.maximum(m_prev, jnp.max(s, axis=1, keepdims=True))
            p = jnp.exp2(s - jnp.concatenate([m_next] * (bk // LANES), axis=1))
            alpha = jnp.exp2(m_prev - m_next)
            pv = jnp.dot(p.astype(v.dtype), v, preferred_element_type=F32)
            acc_ref[h, r0:bq, :] = jnp.concatenate([alpha] * (V_PAD // LANES), axis=1) * acc_ref[h, r0:bq, :] + pv
            m_ref[h, r0:bq, :] = m_next

    scores(0, 0, 0)

    def body(i, carry):
        j = 2 * i
        scores(j + 1, 1, 0)
        accumulate(j, 0, 0, False)
        scores(j + 2, 0, 0)
        accumulate(j + 1, 1, 0, False)
        return carry

    lax.fori_loop(0, qi, body, 0)
    scores(2 * qi + 1, 1, bk)
    accumulate(2 * qi, 0, 0, True)
    accumulate(2 * qi + 1, 1, bk, True)

    for h in range(hpb):
        acc = acc_ref[h]
        g = g_ref[:, h * V_HEAD:(h + 1) * V_HEAD].astype(F32)
        o = acc[:, 0:V_HEAD] / acc[:, V_HEAD:V_HEAD + 1]
        o_ref[:, h * V_HEAD:(h + 1) * V_HEAD] = (o * _silu(g)).astype(o_ref.dtype)


def _attention(q, k, v, proj, g_off, batch, seq, heads, bk, hpb):
    m = q.shape[0]
    bq = 2 * bk
    nq = seq // bq
    assert seq % bq == 0 and g_off % (hpb * V_HEAD) == 0
    g0 = g_off // (hpb * V_HEAD)
    return pl.pallas_call(
        functools.partial(_attn_kernel, bq=bq, bk=bk, hpb=hpb),
        grid=(batch, heads // hpb, nq),
        in_specs=[
            pl.BlockSpec((bq, hpb * QK_PAD), lambda b, h, i: (b * nq + i, h)),
            pl.BlockSpec((seq, hpb * QK_PAD), lambda b, h, i: (b, h)),
            pl.BlockSpec((seq, hpb * V_PAD), lambda b, h, i: (b, h), pipeline_mode=pl.Buffered(1)),
            pl.BlockSpec((bq, hpb * V_HEAD), lambda b, h, i: (b * nq + i, g0 + h)),
        ],
        out_specs=pl.BlockSpec((bq, hpb * V_HEAD), lambda b, h, i: (b * nq + i, h)),
        out_shape=jax.ShapeDtypeStruct((m, heads * V_HEAD), BF16),
        scratch_shapes=[
            pltpu.VMEM((hpb, bq, LANES), F32),
            pltpu.VMEM((hpb, bq, V_PAD), F32),
            pltpu.VMEM((2, hpb, bq, bk), F32),
        ],
        compiler_params=_params("parallel", "parallel", "arbitrary"),
        name="mla_attention",
    )(q, k, v, proj)


def _merge_kernel(ys_ref, ya_ref, ws_ref, wa_ref, gs_ref, ga_ref, o_ref):
    a = jnp.dot(ys_ref[...], ws_ref[...], preferred_element_type=F32)
    b = jnp.dot(ya_ref[...], wa_ref[...], preferred_element_type=F32)
    o = _sigmoid(gs_ref[...].astype(F32)) * a + _sigmoid(ga_ref[...].astype(F32)) * b
    o_ref[...] = o.astype(o_ref.dtype)


def _merge(y_ssm, y_attn, w_s, w_a, proj, off, bm, bn):
    m, ds = y_ssm.shape
    da = y_attn.shape[1]
    d = w_s.shape[1]
    gs0, ga0 = off["gate_ssm"] // bn, off["gate_attn"] // bn
    return pl.pallas_call(
        _merge_kernel,
        grid=(d // bn, m // bm),
        in_specs=[
            pl.BlockSpec((bm, ds), lambda j, i: (i, 0)),
            pl.BlockSpec((bm, da), lambda j, i: (i, 0)),
            pl.BlockSpec((ds, bn), lambda j, i: (0, j)),
            pl.BlockSpec((da, bn), lambda j, i: (0, j)),
            pl.BlockSpec((bm, bn), lambda j, i: (i, gs0 + j)),
            pl.BlockSpec((bm, bn), lambda j, i: (i, ga0 + j)),
        ],
        out_specs=pl.BlockSpec((bm, bn), lambda j, i: (i, j)),
        out_shape=jax.ShapeDtypeStruct((m, d), BF16),
        compiler_params=_params("parallel", "parallel"),
        name="merge_proj",
    )(y_ssm, y_attn, w_s, w_a, proj, proj)


def _out_kernel(a_ref, b_ref, x_ref, nw_ref, o_ref, *, nj, bn, final_norm):
    j = pl.program_id(1)
    hblk = x_ref[...] + jnp.dot(a_ref[...], b_ref[...], preferred_element_type=F32)
    for jj in range(nj):
        @pl.when(j == jj)
        def _(jj=jj):
            o_ref[:, jj * bn:(jj + 1) * bn] = hblk

    if final_norm:
        @pl.when(j == nj - 1)
        def _():
            hf = o_ref[...]
            ms = jnp.mean(hf * hf, axis=-1, keepdims=True)
            o_ref[...] = (hf * lax.rsqrt(ms + EPS)) * nw_ref[...]


def _out_proj(merged, w_out, x, norm_w, final_norm, bm, bn):
    m, k = merged.shape
    n = w_out.shape[1]
    nj = n // bn
    return pl.pallas_call(
        functools.partial(_out_kernel, nj=nj, bn=bn, final_norm=final_norm),
        grid=(m // bm, nj),
        in_specs=[
            pl.BlockSpec((bm, k), lambda i, j: (i, 0)),
            pl.BlockSpec((k, bn), lambda i, j: (0, j)),
            pl.BlockSpec((bm, bn), lambda i, j: (i, j)),
            pl.BlockSpec((1, n), lambda i, j: (0, 0)),
        ],
        out_specs=pl.BlockSpec((bm, n), lambda i, j: (i, 0)),
        out_shape=jax.ShapeDtypeStruct((m, n), F32),
        compiler_params=_params("parallel", "arbitrary"),
        name="out_proj",
    )(merged, w_out, x, norm_w.astype(F32).reshape(1, n))


def _in_proj_layout(n_cols, d_ssm, groups, heads_ssm, q_lora, kv_lora, d_attn, d_model, bn_in):
    gn = groups * SSM_STATE
    sizes = (d_ssm, d_ssm + 2 * gn, heads_ssm, q_lora, kv_lora + QK_ROPE, d_attn, d_model, d_model)
    starts = [0]
    for s in sizes[:-1]:
        starts.append(starts[-1] + s)
    z0, xbc0, dt0, cq0, kv0, ga0, gs0, gt0 = starts
    n_head = -(-(kv0 + kv_lora) // bn_in)
    n_gate = (d_attn + 2 * d_model) // bn_in
    assert (d_attn + 2 * d_model) % bn_in == 0 and n_head * bn_in <= n_cols and gt0 + d_model == n_cols
    tile_cols = [t * bn_in for t in range(n_head)] + [ga0 + t * bn_in for t in range(n_gate)]
    gates0 = n_head * bn_in
    off = {"z": z0, "xs": xbc0, "B": xbc0 + d_ssm, "C": xbc0 + d_ssm + gn, "cq": cq0, "kvc": kv0,
           "g_attn": gates0, "gate_ssm": gates0 + d_attn, "gate_attn": gates0 + d_attn + d_model}
    return tile_cols, off, dt0, kv0 + kv_lora


def kernel(x, norm_in_w, w_in, conv_w, conv_b, dt_bias, a_log, d_skip, ssm_norm_w, q_norm_w, w_uq,
           kv_norm_w, w_ukv, w_branch_ssm, w_branch_attn, w_out, norm_final_w):
    batch, seq, d_model = x.shape
    depth = w_in.shape[0]
    m = batch * seq
    d_ssm = w_branch_ssm.shape[1]
    d_attn = w_branch_attn.shape[1]
    heads_ssm = dt_bias.shape[1]
    groups = (conv_w.shape[2] - d_ssm) // (2 * SSM_STATE)
    gw = (heads_ssm // groups) * SSM_HEADDIM
    q_lora, kv_lora = w_uq.shape[1], w_ukv.shape[1]
    heads = d_attn // V_HEAD
    assert heads_ssm * SSM_HEADDIM == d_ssm and w_uq.shape[2] == heads * (QK_NOPE + QK_ROPE)

    bm_norm = min(256, m)
    bm_in, bn_in = min(1024, m), 512
    bm_qkv, hb_qkv = min(512, seq), 8
    bk_attn, hpb_attn = min(512, seq // 2), 2
    bm_merge, bn_merge = min(512, m), 512
    bm_out, bn_out = min(512, m), 512

    cs1, cs2 = _rope_tables(seq)
    wt_in = jnp.swapaxes(w_in, 1, 2)
    tile_cols, off, dt_col, kr_col = _in_proj_layout(w_in.shape[2], d_ssm, groups, heads_ssm, q_lora, kv_lora,
                                                     d_attn, d_model, bn_in)
    piece = math.gcd(bn_in, *tile_cols[1:])
    h = x.reshape(m, d_model)
    for layer in range(depth):
        u = _rmsnorm(h, norm_in_w[layer], BF16, bm_norm, "norm_in")
        proj = _matmul_wt(u, wt_in, layer, tile_cols, piece, BF16, bm_in, bn_in, "in_proj")
        kr = _kr_proj(u, wt_in, layer, kr_col, bm_in)

        dt, acs, acst = _dt_prep(u, wt_in, layer, dt_col, dt_bias[layer].astype(F32), a_log[layer].astype(F32))
        y_ssm = _ssd(proj, off, dt, acs, acst, conv_w[layer], conv_b[layer], d_skip[layer],
                     ssm_norm_w[layer], batch, seq, groups, d_ssm)

        q = _q_proj(proj, off["cq"], q_norm_w[layer], w_uq[layer], cs1, cs2, seq, heads, bm_qkv, hb_qkv)
        k, v = _kv_proj(proj, off["kvc"], kr, kv_norm_w[layer], w_ukv[layer], cs1, cs2, seq, heads, bm_qkv, hb_qkv)
        y_attn = _attention(q, k, v, proj, off["g_attn"], batch, seq, heads, bk_attn, hpb_attn)

        merged = _merge(y_ssm, y_attn, w_branch_ssm[layer].astype(BF16), w_branch_attn[layer].astype(BF16),
                        proj, off, bm_merge, bn_merge)
        last = layer == depth - 1
        h = _out_proj(merged, w_out[layer].astype(BF16), h, norm_final_w, last, bm_out, bn_out)
    return h.reshape(batch, seq, d_model)
```

```python
import functools
import math

import jax
import jax.numpy as jnp
from jax import lax
from jax.experimental import pallas as pl
from jax.experimental.pallas import tpu as pltpu

F32 = jnp.float32
BF16 = jnp.bfloat16

EPS = 1e-6
SSM_HEADDIM = 64
SSM_STATE = 128
CONV_WIDTH = 4
CHUNK = 128
QK_NOPE = 128
QK_ROPE = 64
V_HEAD = 128
ROPE_THETA = 10000.0

LANES = 128
SUBLANES = 8
PACKED_ROWS = 16
VMEM_LIMIT_BYTES = 56 * 1024 * 1024
QK_PAD = 2 * LANES
V_PAD = 2 * LANES
MASK_VALUE = -1e30
LOG2E = math.log2(math.e)


def _params(*sem):
    return pltpu.CompilerParams(dimension_semantics=sem, vmem_limit_bytes=VMEM_LIMIT_BYTES)


def _sigmoid(x):
    return 0.5 + 0.5 * jnp.tanh(0.5 * x)


def _silu(x):
    h = 0.5 * x
    return h + h * jnp.tanh(h)


def _rmsnorm_kernel(x_ref, w_ref, o_ref):
    x = x_ref[...].astype(F32)
    ms = jnp.mean(x * x, axis=-1, keepdims=True)
    o_ref[...] = ((x * lax.rsqrt(ms + EPS)) * w_ref[...]).astype(o_ref.dtype)


def _rmsnorm(x, w, out_dtype, bm, name):
    m, d = x.shape
    return pl.pallas_call(
        _rmsnorm_kernel,
        grid=(m // bm,),
        in_specs=[pl.BlockSpec((bm, d), lambda i: (i, 0)), pl.BlockSpec((1, d), lambda i: (0, 0))],
        out_specs=pl.BlockSpec((bm, d), lambda i: (i, 0)),
        out_shape=jax.ShapeDtypeStruct((m, d), out_dtype),
        compiler_params=_params("parallel"),
        name=name,
    )(x, w.reshape(1, d))


def _matmul_kernel(a_ref, b_ref, o_ref):
    o_ref[...] = jnp.dot(a_ref[...], b_ref[...], preferred_element_type=F32).astype(o_ref.dtype)


def _matmul(a, b, out_dtype, bm, bn, name):
    m, k = a.shape
    n = b.shape[1]
    return pl.pallas_call(
        _matmul_kernel,
        grid=(n // bn, m // bm),
        in_specs=[pl.BlockSpec((bm, k), lambda j, i: (i, 0)), pl.BlockSpec((k, bn), lambda j, i: (0, j))],
        out_specs=pl.BlockSpec((bm, bn), lambda j, i: (i, j)),
        out_shape=jax.ShapeDtypeStruct((m, n), out_dtype),
        compiler_params=_params("parallel", "parallel"),
        name=name,
    )(a, b)


_NT = (((1,), (1,)), ((), ()))


def _matmul_wt_kernel(*refs, npiece, piece):
    a_ref, w_refs, o_ref, w16_ref = refs[1], refs[2:2 + npiece], refs[2 + npiece], refs[3 + npiece]

    @pl.when(pl.program_id(1) == 0)
    def _():
        wt_tile = jnp.concatenate([w_ref[...] for w_ref in w_refs], axis=0)
        w16_ref[...] = wt_tile.T.astype(BF16)

    o_ref[...] = jnp.dot(a_ref[...], w16_ref[...], preferred_element_type=F32).astype(o_ref.dtype)


def _matmul_wt(a, wt, layer, tile_rows, piece, out_dtype, bm, bn, name):
    m, k = a.shape
    npiece = bn // piece
    n_tiles = len(tile_rows)
    starts = jnp.asarray(tile_rows, jnp.int32) // piece
    assert all(r % piece == 0 for r in tile_rows)

    def piece_spec(t):
        return pl.BlockSpec((None, piece, k), lambda j, i, s: (layer, s[j] + t, 0))

    return pl.pallas_call(
        functools.partial(_matmul_wt_kernel, npiece=npiece, piece=piece),
        grid_spec=pltpu.PrefetchScalarGridSpec(
            num_scalar_prefetch=1,
            grid=(n_tiles, m // bm),
            in_specs=[pl.BlockSpec((bm, k), lambda j, i, s: (i, 0))] + [piece_spec(t) for t in range(npiece)],
            out_specs=pl.BlockSpec((bm, bn), lambda j, i, s: (i, j)),
            scratch_shapes=[pltpu.VMEM((k, bn), BF16)],
        ),
        out_shape=jax.ShapeDtypeStruct((m, n_tiles * bn), out_dtype),
        compiler_params=_params("parallel", "arbitrary"),
        name=name,
    )(starts, a, *([wt] * npiece))


def _dt_kernel(u_ref, wt_ref, bias_ref, alog_ref, dt_ref, acs_ref, acst_ref):
    raw = lax.dot_general(u_ref[...], wt_ref[...].astype(BF16), _NT, preferred_element_type=F32) + bias_ref[...]
    dt = jnp.maximum(raw, 0.0) + jnp.log1p(jnp.exp(-jnp.abs(raw)))
    a = dt * (-jnp.exp(alog_ref[...]))
    li = lax.broadcasted_iota(jnp.int32, (CHUNK, CHUNK), 0)
    ki = lax.broadcasted_iota(jnp.int32, (CHUNK, CHUNK), 1)
    tril = (ki <= li).astype(F32)
    acs = jnp.dot(tril, a, precision=lax.Precision.HIGHEST, preferred_element_type=F32)
    dt_ref[...] = dt
    acs_ref[...] = acs
    acst_ref[0] = acs.T


def _dt_prep(u, wt, layer, dt_row0, dt_bias, a_log):
    m, d = u.shape
    h = dt_bias.shape[0]
    nc = m // CHUNK
    assert dt_row0 % h == 0
    return pl.pallas_call(
        _dt_kernel,
        grid=(nc,),
        in_specs=[
            pl.BlockSpec((CHUNK, d), lambda i: (i, 0)),
            pl.BlockSpec((None, h, d), lambda i: (layer, dt_row0 // h, 0)),
            pl.BlockSpec((1, h), lambda i: (0, 0)),
            pl.BlockSpec((1, h), lambda i: (0, 0)),
        ],
        out_specs=[
            pl.BlockSpec((CHUNK, h), lambda i: (i, 0)),
            pl.BlockSpec((CHUNK, h), lambda i: (i, 0)),
            pl.BlockSpec((1, h, CHUNK), lambda i: (i, 0, 0)),
        ],
        out_shape=[
            jax.ShapeDtypeStruct((m, h), F32),
            jax.ShapeDtypeStruct((m, h), F32),
            jax.ShapeDtypeStruct((nc, h, CHUNK), F32),
        ],
        compiler_params=_params("parallel"),
        name="dt_prep",
    )(u, wt, dt_bias.reshape(1, h), a_log.reshape(1, h))


def _kr_kernel(u_ref, wt_ref, o_ref):
    w = wt_ref[...]
    half = QK_ROPE // 2
    w_all = jnp.concatenate([w, w[half:], w[:half]], axis=0).astype(BF16)
    o_ref[...] = lax.dot_general(u_ref[...], w_all, _NT, preferred_element_type=F32).astype(o_ref.dtype)


def _rope_tile(t, cs1, cs2):
    return t * cs1 + pltpu.roll(t, LANES // 2, 1) * cs2


def _kr_proj(u, wt, layer, kr_row0, bm):
    m, d = u.shape
    assert kr_row0 % QK_ROPE == 0
    return pl.pallas_call(
        _kr_kernel,
        grid=(m // bm,),
        in_specs=[
            pl.BlockSpec((bm, d), lambda i: (i, 0)),
            pl.BlockSpec((None, QK_ROPE, d), lambda i: (layer, kr_row0 // QK_ROPE, 0)),
        ],
        out_specs=pl.BlockSpec((bm, LANES), lambda i: (i, 0)),
        out_shape=jax.ShapeDtypeStruct((m, LANES), BF16),
        compiler_params=_params("parallel"),
        name="kr_proj",
    )(u, wt)


def _split_expand(v, onehot2_ref):
    hi = v.astype(BF16)
    mid = (v - hi.astype(F32)).astype(BF16)
    return jnp.dot(jnp.concatenate([hi, mid], axis=1), onehot2_ref[...], preferred_element_type=F32)


def _ssd_kernel(xs_ref, b_ref, c_ref, z_ref, dt_ref, acs_ref, acst_ref, cw_ref, cbias_ref, shift_ref,
                rexp_ref, rexp2_ref, eexp2_ref, dskip_ref, nw_ref,
                y_ref, pad_ref, state_ref, ydiag_ref, *, r, cps):
    gw = r * SSM_HEADDIM
    n = SSM_STATE
    halo = PACKED_ROWS
    rows_step = cps * CHUNK

    @pl.when(pl.program_id(2) == 0)
    def _():
        pad_ref[0:halo, :] = jnp.zeros((halo, gw + 2 * n), BF16)
        state_ref[...] = jnp.zeros_like(state_ref)

    pad_ref[halo:halo + rows_step, 0:gw] = xs_ref[...]
    pad_ref[halo:halo + rows_step, gw:gw + n] = b_ref[...]
    pad_ref[halo:halo + rows_step, gw + n:gw + 2 * n] = c_ref[...]
    li = lax.broadcasted_iota(jnp.int32, (CHUNK, CHUNK), 0)
    si = lax.broadcasted_iota(jnp.int32, (CHUNK, CHUNK), 1)
    causal = si <= li
    low_half = si < SSM_HEADDIM

    for sub in range(cps):
        r0 = sub * CHUNK
        window = pad_ref[r0:r0 + halo + CHUNK, :]
        shifted = jnp.dot(shift_ref[...], window, preferred_element_type=F32)
        conv = cbias_ref[0]
        for k in range(CONV_WIDTH - 1):
            conv = conv + shifted[k * CHUNK:(k + 1) * CHUNK] * cw_ref[0, k:k + 1, :]
        conv = conv + window[halo:halo + CHUNK, :].astype(F32) * cw_ref[0, CONV_WIDTH - 1:CONV_WIDTH, :]
        xc = _silu(conv[:, 0:gw])
        bc = _silu(conv[:, gw:gw + n])
        cc = _silu(conv[:, gw + n:gw + 2 * n])

        cb16 = cc.astype(BF16)
        bb16 = bc.astype(BF16)
        cbm = lax.dot_general(cb16, bb16, _NT, preferred_element_type=F32)

        acs = acs_ref[r0:r0 + CHUNK, :]
        acs_last = acs[CHUNK - 1:CHUNK, :]
        per_head = jnp.concatenate([dt_ref[r0:r0 + CHUNK, :], jnp.exp(acs), jnp.exp(acs_last - acs)], axis=0)
        ex = jnp.dot(per_head.astype(BF16), rexp_ref[...], preferred_element_type=F32)
        dt_e = ex[0:CHUNK]
        eacs_e = ex[CHUNK:2 * CHUNK]
        dst_e = ex[2 * CHUNK:3 * CHUNK]
        last_e = _split_expand(jnp.broadcast_to(acs_last, (PACKED_ROWS, acs.shape[1])), rexp2_ref)
        cd_e = jnp.exp(last_e[0:1, :])
        col_all = _split_expand(acs, eexp2_ref)

        x32 = xc * dt_e
        x16 = x32.astype(BF16)
        for p in range(r // 2):
            xp = x16[:, p * LANES:(p + 1) * LANES]
            ys = []
            for q in range(2):
                j = 2 * p + q
                seg = col_all[:, j * LANES:(j + 1) * LANES] - acst_ref[sub, j:j + 1, :]
                mj = (jnp.exp(jnp.where(causal, seg, -jnp.inf)) * cbm).astype(BF16)
                ys.append(jnp.dot(mj, xp, preferred_element_type=F32))
            ydiag_ref[sub, :, p * LANES:(p + 1) * LANES] = jnp.where(low_half, ys[0], ys[1])

        st = state_ref[...]
        y_off = jnp.dot(cb16, st.astype(BF16), preferred_element_type=F32) * eacs_e
        xd16 = (x32 * dst_e).astype(BF16)
        state_ref[...] = st * cd_e + jnp.dot(bc.T.astype(BF16), xd16, preferred_element_type=F32)

        y = ydiag_ref[sub] + y_off + xc * dskip_ref[...]
        gv = y * _silu(z_ref[r0:r0 + CHUNK, :].astype(F32))
        ms = jnp.mean(gv * gv, axis=-1, keepdims=True)
        y_ref[r0:r0 + CHUNK, :] = ((gv * lax.rsqrt(ms + EPS)) * nw_ref[...]).astype(y_ref.dtype)

    pad_ref[0:halo, :] = pad_ref[rows_step:rows_step + halo, :]


def _ssd(proj, off, dt, acs, acst, conv_w, conv_b, d_skip, ssm_norm_w, batch, seq, groups, d_ssm, cps):
    m = proj.shape[0]
    heads = dt.shape[1]
    r = heads // groups
    gw = r * SSM_HEADDIM
    n = SSM_STATE
    nc = seq // CHUNK
    assert heads == LANES and r % 2 == 0 and gw % LANES == 0 and seq % (cps * CHUNK) == 0

    head_of_col = jnp.arange(d_ssm, dtype=jnp.int32) // SSM_HEADDIM
    rexp = (jnp.arange(heads, dtype=jnp.int32)[:, None] == head_of_col[None, :]).astype(BF16)
    head_of_tile = jnp.arange(heads * LANES, dtype=jnp.int32) // LANES
    eexp = (jnp.arange(heads, dtype=jnp.int32)[:, None] == head_of_tile[None, :]).astype(BF16)
    rexp2 = jnp.concatenate([rexp] * 2, axis=0)
    eexp2 = jnp.concatenate([eexp] * 2, axis=0)
    dskip_e = jnp.repeat(d_skip.astype(F32), SSM_HEADDIM).reshape(1, d_ssm)

    def per_group(a):
        rows = a.shape[0]
        xs = a[:, :d_ssm].reshape(rows, groups, gw)
        bs = a[:, d_ssm:d_ssm + groups * n].reshape(rows, groups, n)
        cs = a[:, d_ssm + groups * n:].reshape(rows, groups, n)
        return jnp.concatenate([xs, bs, cs], axis=2).transpose(1, 0, 2)

    cw = per_group(conv_w.astype(F32))
    cb = per_group(conv_b.astype(F32).reshape(1, -1))
    halo = PACKED_ROWS
    t_idx = jnp.arange((CONV_WIDTH - 1) * CHUNK, dtype=jnp.int32)
    src = halo + (t_idx % CHUNK) - (CONV_WIDTH - 1) + t_idx // CHUNK
    shift = (src[:, None] == jnp.arange(halo + CHUNK, dtype=jnp.int32)[None, :]).astype(BF16)

    ns = nc // cps
    rows = cps * CHUNK

    def row(b, g, c):
        return b * ns + c

    xs0, b0, c0, z0 = off["xs"] // gw, off["B"] // n, off["C"] // n, off["z"] // gw
    cols = gw + 2 * n
    in_specs = [
        pl.BlockSpec((rows, gw), lambda b, g, c: (row(b, g, c), xs0 + g)),
        pl.BlockSpec((rows, n), lambda b, g, c: (row(b, g, c), b0 + g)),
        pl.BlockSpec((rows, n), lambda b, g, c: (row(b, g, c), c0 + g)),
        pl.BlockSpec((rows, gw), lambda b, g, c: (row(b, g, c), z0 + g)),
        pl.BlockSpec((rows, heads), lambda b, g, c: (row(b, g, c), 0)),
        pl.BlockSpec((rows, heads), lambda b, g, c: (row(b, g, c), 0)),
        pl.BlockSpec((cps, r, CHUNK), lambda b, g, c: (row(b, g, c), g, 0)),
        pl.BlockSpec((1, CONV_WIDTH, cols), lambda b, g, c: (g, 0, 0)),
        pl.BlockSpec((1, 1, cols), lambda b, g, c: (g, 0, 0)),
        pl.BlockSpec(((CONV_WIDTH - 1) * CHUNK, halo + CHUNK), lambda b, g, c: (0, 0)),
        pl.BlockSpec((heads, gw), lambda b, g, c: (0, g)),
        pl.BlockSpec((2 * heads, gw), lambda b, g, c: (0, g)),
        pl.BlockSpec((2 * heads, r * LANES), lambda b, g, c: (0, g)),
        pl.BlockSpec((1, gw), lambda b, g, c: (0, g)),
        pl.BlockSpec((1, gw), lambda b, g, c: (0, g)),
    ]
    return pl.pallas_call(
        functools.partial(_ssd_kernel, r=r, cps=cps),
        grid=(batch, groups, ns),
        in_specs=in_specs,
        out_specs=pl.BlockSpec((rows, gw), lambda b, g, c: (row(b, g, c), g)),
        out_shape=jax.ShapeDtypeStruct((m, d_ssm), BF16),
        scratch_shapes=[
            pltpu.VMEM((halo + rows, cols), BF16),
            pltpu.VMEM((n, gw), F32),
            pltpu.VMEM((cps, CHUNK, gw), F32),
        ],
        compiler_params=_params("parallel", "parallel", "arbitrary"),
        name="ssd_scan",
    )(proj, proj, proj, proj, dt, acs, acst, cw, cb, shift,
      rexp, rexp2, eexp2, dskip_e, ssm_norm_w.astype(F32).reshape(1, d_ssm))


def _normed(x_refs, w_ref):
    x = jnp.concatenate([r[...] for r in x_refs], axis=1).astype(F32)
    ms = jnp.mean(x * x, axis=-1, keepdims=True)
    return ((x * lax.rsqrt(ms + EPS)) * w_ref[...]).astype(BF16)


def _piece_specs(offset, width, bm):
    piece = math.gcd(offset, width)
    assert piece % LANES == 0
    first = offset // piece
    return [pl.BlockSpec((bm, piece), lambda j, i, t=t: (i, first + t)) for t in range(width // piece)]


def _q_kernel(*refs, hb, scale):
    nw_ref, wq_ref, cs1_ref, cs2_ref, o_ref = refs[-5:]
    cqn = _normed(refs[:-5], nw_ref)
    q = jnp.dot(cqn, wq_ref[...], preferred_element_type=F32)
    cs1 = cs1_ref[...] * scale
    cs2 = cs2_ref[...] * scale
    for h in range(hb):
        c0 = h * QK_PAD
        o_ref[:, c0:c0 + LANES] = (q[:, c0:c0 + LANES] * scale).astype(o_ref.dtype)
        o_ref[:, c0 + LANES:c0 + QK_PAD] = _rope_tile(q[:, c0 + LANES:c0 + QK_PAD], cs1, cs2).astype(o_ref.dtype)


def _kv_kernel(*refs, hb):
    kr_ref, nw_ref, w_ref, cs1_ref, cs2_ref, k_ref, v_ref = refs[-7:]
    ckv = _normed(refs[:-7], nw_ref)
    kv = jnp.dot(ckv, w_ref[...], preferred_element_type=F32)
    roped = _rope_tile(kr_ref[...].astype(F32), cs1_ref[...], cs2_ref[...]).astype(k_ref.dtype)
    lane = lax.broadcasted_iota(jnp.int32, roped.shape, 1)
    ones_col = jnp.where(lane == 0, 1.0, 0.0).astype(v_ref.dtype)
    for h in range(hb):
        c0 = h * QK_PAD
        k_ref[:, c0:c0 + LANES] = kv[:, c0:c0 + LANES].astype(k_ref.dtype)
        k_ref[:, c0 + LANES:c0 + QK_PAD] = roped
        v_ref[:, c0:c0 + V_HEAD] = kv[:, c0 + LANES:c0 + QK_PAD].astype(v_ref.dtype)
        v_ref[:, c0 + V_HEAD:c0 + V_PAD] = ones_col


def _rope_tables(seq):
    half = QK_ROPE // 2
    inv_freq = ROPE_THETA ** (-jnp.arange(0, half, dtype=F32) / half)
    ang = jnp.arange(seq, dtype=jnp.int32).astype(F32)[:, None] * inv_freq[None, :]
    cos, sin = jnp.cos(ang), jnp.sin(ang)
    zeros = jnp.zeros((seq, LANES - QK_ROPE), F32)
    return jnp.concatenate([cos, cos, zeros], axis=1), jnp.concatenate([-sin, sin, zeros], axis=1)


def _q_proj(proj, cq_off, q_norm_w, w_uq, cs1, cs2, seq, heads, bm, hb):
    m = proj.shape[0]
    ql = w_uq.shape[0]
    w = w_uq.reshape(ql, heads, QK_NOPE + QK_ROPE)
    nope, rp = w[:, :, :QK_NOPE], w[:, :, QK_NOPE:]
    half = QK_ROPE // 2
    wq = jnp.concatenate([nope, rp, rp[:, :, half:], rp[:, :, :half]], axis=2).reshape(ql, heads * QK_PAD).astype(BF16)
    nb = seq // bm
    scale = LOG2E / math.sqrt(QK_NOPE + QK_ROPE)
    cq_specs = _piece_specs(cq_off, ql, bm)
    return pl.pallas_call(
        functools.partial(_q_kernel, hb=hb, scale=scale),
        grid=(heads // hb, m // bm),
        in_specs=cq_specs + [
            pl.BlockSpec((1, ql), lambda j, i: (0, 0)),
            pl.BlockSpec((ql, hb * QK_PAD), lambda j, i: (0, j)),
            pl.BlockSpec((bm, LANES), lambda j, i: (i % nb, 0)),
            pl.BlockSpec((bm, LANES), lambda j, i: (i % nb, 0)),
        ],
        out_specs=pl.BlockSpec((bm, hb * QK_PAD), lambda j, i: (i, j)),
        out_shape=jax.ShapeDtypeStruct((m, heads * QK_PAD), BF16),
        compiler_params=_params("parallel", "parallel"),
        name="q_proj",
    )(*([proj] * len(cq_specs)), q_norm_w.astype(F32).reshape(1, ql), wq, cs1, cs2)


def _kv_proj(proj, kvc_off, kr, kv_norm_w, w_ukv, cs1, cs2, seq, heads, bm, hb):
    m = proj.shape[0]
    kvl = w_ukv.shape[0]
    nb = seq // bm
    kvc_specs = _piece_specs(kvc_off, kvl, bm)
    return pl.pallas_call(
        functools.partial(_kv_kernel, hb=hb),
        grid=(heads // hb, m // bm),
        in_specs=kvc_specs + [
            pl.BlockSpec((bm, LANES), lambda j, i: (i, 0)),
            pl.BlockSpec((1, kvl), lambda j, i: (0, 0)),
            pl.BlockSpec((kvl, hb * QK_PAD), lambda j, i: (0, j)),
            pl.BlockSpec((bm, LANES), lambda j, i: (i % nb, 0)),
            pl.BlockSpec((bm, LANES), lambda j, i: (i % nb, 0)),
        ],
        out_specs=[
            pl.BlockSpec((bm, hb * QK_PAD), lambda j, i: (i, j)),
            pl.BlockSpec((bm, hb * V_PAD), lambda j, i: (i, j)),
        ],
        out_shape=[
            jax.ShapeDtypeStruct((m, heads * QK_PAD), BF16),
            jax.ShapeDtypeStruct((m, heads * V_PAD), BF16),
        ],
        compiler_params=_params("parallel", "parallel"),
        name="kv_proj",
    )(*([proj] * len(kvc_specs)), kr, kv_norm_w.astype(F32).reshape(1, kvl),
      w_ukv.astype(BF16), cs1, cs2)


def _attn_kernel(q_ref, k_ref, v_ref, g_ref, o_ref, m_ref, acc_ref, s_ref, *, bq, bk, hpb):
    qi = pl.program_id(2)
    m_ref[...] = jnp.full(m_ref.shape, MASK_VALUE, F32)
    acc_ref[...] = jnp.zeros_like(acc_ref)

    def scores(j, slot, r0):
        start = pl.multiple_of(j * bk, bk)
        for h in range(hpb):
            s_ref[slot, h, r0:bq, :] = lax.dot_general(
                q_ref[r0:bq, h * QK_PAD:(h + 1) * QK_PAD], k_ref[pl.ds(start, bk), h * QK_PAD:(h + 1) * QK_PAD],
                _NT, preferred_element_type=F32)

    def accumulate(j, slot, r0, masked):
        start = pl.multiple_of(j * bk, bk)
        rows = bq - r0
        for h in range(hpb):
            s = s_ref[slot, h, r0:bq, :]
            v = v_ref[pl.ds(start, bk), h * V_PAD:(h + 1) * V_PAD]
            if masked:
                ri = lax.broadcasted_iota(jnp.int32, (rows, bk), 0)
                ci = lax.broadcasted_iota(jnp.int32, (rows, bk), 1)
                s = jnp.where(ci <= ri, s, MASK_VALUE)
            m_prev = m_ref[h, r0:bq, :]
            m_next = jnp.maximum(m_prev, jnp.max(s, axis=1, keepdims=True))
            p = jnp.exp2(s - jnp.concatenate([m_next] * (bk // LANES), axis=1))
            alpha = jnp.exp2(m_prev - m_next)
            pv = jnp.dot(p.astype(v.dtype), v, preferred_element_type=F32)
            acc_ref[h, r0:bq, :] = jnp.concatenate([alpha] * (V_PAD // LANES), axis=1) * acc_ref[h, r0:bq, :] + pv
            m_ref[h, r0:bq, :] = m_next

    scores(0, 0, 0)

    def body(i, carry):
        j = 2 * i
        scores(j + 1, 1, 0)
        accumulate(j, 0, 0, False)
        scores(j + 2, 0, 0)
        accumulate(j + 1, 1, 0, False)
        return carry

    lax.fori_loop(0, qi, body, 0)
    scores(2 * qi + 1, 1, bk)
    accumulate(2 * qi, 0, 0, True)
    accumulate(2 * qi + 1, 1, bk, True)

    for h in range(hpb):
        acc = acc_ref[h]
        g = g_ref[:, h * V_HEAD:(h + 1) * V_HEAD].astype(F32)
        o = acc[:, 0:V_HEAD] / acc[:, V_HEAD:V_HEAD + 1]
        o_ref[:, h * V_HEAD:(h + 1) * V_HEAD] = (o * _silu(g)).astype(o_ref.dtype)


def _attention(q, k, v, proj, g_off, batch, seq, heads, bk, hpb):
    m = q.shape[0]
    bq = 2 * bk
    nq = seq // bq
    assert seq % bq == 0 and g_off % (hpb * V_HEAD) == 0
    g0 = g_off // (hpb * V_HEAD)
    return pl.pallas_call(
        functools.partial(_attn_kernel, bq=bq, bk=bk, hpb=hpb),
        grid=(batch, heads // hpb, nq),
        in_specs=[
            pl.BlockSpec((bq, hpb * QK_PAD), lambda b, h, i: (b * nq + i, h)),
            pl.BlockSpec((seq, hpb * QK_PAD), lambda b, h, i: (b, h)),
            pl.BlockSpec((seq, hpb * V_PAD), lambda b, h, i: (b, h), pipeline_mode=pl.Buffered(1)),
            pl.BlockSpec((bq, hpb * V_HEAD), lambda b, h, i: (b * nq + i, g0 + h)),
        ],
        out_specs=pl.BlockSpec((bq, hpb * V_HEAD), lambda b, h, i: (b * nq + i, h)),
        out_shape=jax.ShapeDtypeStruct((m, heads * V_HEAD), BF16),
        scratch_shapes=[
            pltpu.VMEM((hpb, bq, LANES), F32),
            pltpu.VMEM((hpb, bq, V_PAD), F32),
            pltpu.VMEM((2, hpb, bq, bk), F32),
        ],
        compiler_params=_params("parallel", "parallel", "arbitrary"),
        name="mla_attention",
    )(q, k, v, proj)


def _merge_kernel(ys_ref, ya_ref, ws_ref, wa_ref, gs_ref, ga_ref, o_ref):
    a = jnp.dot(ys_ref[...], ws_ref[...], preferred_element_type=F32)
    b = jnp.dot(ya_ref[...], wa_ref[...], preferred_element_type=F32)
    o = _sigmoid(gs_ref[...].astype(F32)) * a + _sigmoid(ga_ref[...].astype(F32)) * b
    o_ref[...] = o.astype(o_ref.dtype)


def _merge(y_ssm, y_attn, w_s, w_a, proj, off, bm, bn):
    m, ds = y_ssm.shape
    da = y_attn.shape[1]
    d = w_s.shape[1]
    gs0, ga0 = off["gate_ssm"] // bn, off["gate_attn"] // bn
    return pl.pallas_call(
        _merge_kernel,
        grid=(d // bn, m // bm),
        in_specs=[
            pl.BlockSpec((bm, ds), lambda j, i: (i, 0)),
            pl.BlockSpec((bm, da), lambda j, i: (i, 0)),
            pl.BlockSpec((ds, bn), lambda j, i: (0, j)),
            pl.BlockSpec((da, bn), lambda j, i: (0, j)),
            pl.BlockSpec((bm, bn), lambda j, i: (i, gs0 + j)),
            pl.BlockSpec((bm, bn), lambda j, i: (i, ga0 + j)),
        ],
        out_specs=pl.BlockSpec((bm, bn), lambda j, i: (i, j)),
        out_shape=jax.ShapeDtypeStruct((m, d), BF16),
        compiler_params=_params("parallel", "parallel"),
        name="merge_proj",
    )(y_ssm, y_attn, w_s, w_a, proj, proj)


def _out_kernel(a_ref, b_ref, x_ref, nw_ref, o_ref, *, nj, bn, final_norm):
    j = pl.program_id(1)
    hblk = x_ref[...] + jnp.dot(a_ref[...], b_ref[...], preferred_element_type=F32)
    for jj in range(nj):
        @pl.when(j == jj)
        def _(jj=jj):
            o_ref[:, jj * bn:(jj + 1) * bn] = hblk

    if final_norm:
        @pl.when(j == nj - 1)
        def _():
            hf = o_ref[...]
            ms = jnp.mean(hf * hf, axis=-1, keepdims=True)
            o_ref[...] = (hf * lax.rsqrt(ms + EPS)) * nw_ref[...]


def _out_proj(merged, w_out, x, norm_w, final_norm, bm, bn):
    m, k = merged.shape
    n = w_out.shape[1]
    nj = n // bn
    return pl.pallas_call(
        functools.partial(_out_kernel, nj=nj, bn=bn, final_norm=final_norm),
        grid=(m // bm, nj),
        in_specs=[
            pl.BlockSpec((bm, k), lambda i, j: (i, 0)),
            pl.BlockSpec((k, bn), lambda i, j: (0, j)),
            pl.BlockSpec((bm, bn), lambda i, j: (i, j)),
            pl.BlockSpec((1, n), lambda i, j: (0, 0)),
        ],
        out_specs=pl.BlockSpec((bm, n), lambda i, j: (i, 0)),
        out_shape=jax.ShapeDtypeStruct((m, n), F32),
        compiler_params=_params("parallel", "arbitrary"),
        name="out_proj",
    )(merged, w_out, x, norm_w.astype(F32).reshape(1, n))


def _in_proj_layout(n_cols, d_ssm, groups, heads_ssm, q_lora, kv_lora, d_attn, d_model, bn_in):
    gn = groups * SSM_STATE
    sizes = (d_ssm, d_ssm + 2 * gn, heads_ssm, q_lora, kv_lora + QK_ROPE, d_attn, d_model, d_model)
    starts = [0]
    for s in sizes[:-1]:
        starts.append(starts[-1] + s)
    z0, xbc0, dt0, cq0, kv0, ga0, gs0, gt0 = starts
    n_head = -(-(kv0 + kv_lora) // bn_in)
    n_gate = (d_attn + 2 * d_model) // bn_in
    assert (d_attn + 2 * d_model) % bn_in == 0 and n_head * bn_in <= n_cols and gt0 + d_model == n_cols
    tile_cols = [t * bn_in for t in range(n_head)] + [ga0 + t * bn_in for t in range(n_gate)]
    gates0 = n_head * bn_in
    off = {"z": z0, "xs": xbc0, "B": xbc0 + d_ssm, "C": xbc0 + d_ssm + gn, "cq": cq0, "kvc": kv0,
           "g_attn": gates0, "gate_ssm": gates0 + d_attn, "gate_attn": gates0 + d_attn + d_model}
    return tile_cols, off, dt0, kv0 + kv_lora


def kernel(x, norm_in_w, w_in, conv_w, conv_b, dt_bias, a_log, d_skip, ssm_norm_w, q_norm_w, w_uq,
           kv_norm_w, w_ukv, w_branch_ssm, w_branch_attn, w_out, norm_final_w):
    batch, seq, d_model = x.shape
    depth = w_in.shape[0]
    m = batch * seq
    d_ssm = w_branch_ssm.shape[1]
    d_attn = w_branch_attn.shape[1]
    heads_ssm = dt_bias.shape[1]
    groups = (conv_w.shape[2] - d_ssm) // (2 * SSM_STATE)
    gw = (heads_ssm // groups) * SSM_HEADDIM
    q_lora, kv_lora = w_uq.shape[1], w_ukv.shape[1]
    heads = d_attn // V_HEAD
    assert heads_ssm * SSM_HEADDIM == d_ssm and w_uq.shape[2] == heads * (QK_NOPE + QK_ROPE)

    bm_norm = min(256, m)
    bm_in, bn_in = min(1024, m), 512
    cps_ssd = 4
    bm_qkv, hb_qkv = min(512, seq), 8
    bk_attn, hpb_attn = min(512, seq // 2), 2
    bm_merge, bn_merge = min(512, m), 512
    bm_out, bn_out = min(512, m), 512

    cs1, cs2 = _rope_tables(seq)
    wt_in = jnp.swapaxes(w_in, 1, 2)
    tile_cols, off, dt_col, kr_col = _in_proj_layout(w_in.shape[2], d_ssm, groups, heads_ssm, q_lora, kv_lora,
                                                     d_attn, d_model, bn_in)
    piece = math.gcd(bn_in, *tile_cols[1:])
    h = x.reshape(m, d_model)
    for layer in range(depth):
        u = _rmsnorm(h, norm_in_w[layer], BF16, bm_norm, "norm_in")
        proj = _matmul_wt(u, wt_in, layer, tile_cols, piece, BF16, bm_in, bn_in, "in_proj")
        kr = _kr_proj(u, wt_in, layer, kr_col, bm_in)

        dt, acs, acst = _dt_prep(u, wt_in, layer, dt_col, dt_bias[layer].astype(F32), a_log[layer].astype(F32))
        y_ssm = _ssd(proj, off, dt, acs, acst, conv_w[layer], conv_b[layer], d_skip[layer],
                     ssm_norm_w[layer], batch, seq, groups, d_ssm, cps_ssd)

        q = _q_proj(proj, off["cq"], q_norm_w[layer], w_uq[layer], cs1, cs2, seq, heads, bm_qkv, hb_qkv)
        k, v = _kv_proj(proj, off["kvc"], kr, kv_norm_w[layer], w_ukv[layer], cs1, cs2, seq, heads, bm_qkv, hb_qkv)
        y_attn = _attention(q, k, v, proj, off["g_attn"], batch, seq, heads, bk_attn, hpb_attn)

        merged = _merge(y_ssm, y_attn, w_branch_ssm[layer].astype(BF16), w_branch_attn[layer].astype(BF16),
                        proj, off, bm_merge, bn_merge)
        last = layer == depth - 1
        h = _out_proj(merged, w_out[layer].astype(BF16), h, norm_final_w, last, bm_out, bn_out)
    return h.reshape(batch, seq, d_model)
```

```python
import functools
import math

import jax
import jax.numpy as jnp
from jax import lax
from jax.experimental import pallas as pl
from jax.experimental.pallas import tpu as pltpu

F32 = jnp.float32
BF16 = jnp.bfloat16

EPS = 1e-6
SSM_HEADDIM = 64
SSM_STATE = 128
CONV_WIDTH = 4
CHUNK = 128
QK_NOPE = 128
QK_ROPE = 64
V_HEAD = 128
ROPE_THETA = 10000.0

LANES = 128
SUBLANES = 8
PACKED_ROWS = 16
VMEM_LIMIT_BYTES = 56 * 1024 * 1024
QK_PAD = 2 * LANES
V_PAD = 2 * LANES
MASK_VALUE = -1e30
LOG2E = math.log2(math.e)


def _params(*sem):
    return pltpu.CompilerParams(dimension_semantics=sem, vmem_limit_bytes=VMEM_LIMIT_BYTES)


def _sigmoid(x):
    return 0.5 + 0.5 * jnp.tanh(0.5 * x)


def _silu(x):
    h = 0.5 * x
    return h + h * jnp.tanh(h)


def _rmsnorm_kernel(x_ref, w_ref, o_ref):
    x = x_ref[...].astype(F32)
    ms = jnp.mean(x * x, axis=-1, keepdims=True)
    o_ref[...] = ((x * lax.rsqrt(ms + EPS)) * w_ref[...]).astype(o_ref.dtype)


def _rmsnorm(x, w, out_dtype, bm, name):
    m, d = x.shape
    return pl.pallas_call(
        _rmsnorm_kernel,
        grid=(m // bm,),
        in_specs=[pl.BlockSpec((bm, d), lambda i: (i, 0)), pl.BlockSpec((1, d), lambda i: (0, 0))],
        out_specs=pl.BlockSpec((bm, d), lambda i: (i, 0)),
        out_shape=jax.ShapeDtypeStruct((m, d), out_dtype),
        compiler_params=_params("parallel"),
        name=name,
    )(x, w.reshape(1, d))


def _matmul_kernel(a_ref, b_ref, o_ref):
    o_ref[...] = jnp.dot(a_ref[...], b_ref[...], preferred_element_type=F32).astype(o_ref.dtype)


def _matmul(a, b, out_dtype, bm, bn, name):
    m, k = a.shape
    n = b.shape[1]
    return pl.pallas_call(
        _matmul_kernel,
        grid=(n // bn, m // bm),
        in_specs=[pl.BlockSpec((bm, k), lambda j, i: (i, 0)), pl.BlockSpec((k, bn), lambda j, i: (0, j))],
        out_specs=pl.BlockSpec((bm, bn), lambda j, i: (i, j)),
        out_shape=jax.ShapeDtypeStruct((m, n), out_dtype),
        compiler_params=_params("parallel", "parallel"),
        name=name,
    )(a, b)


_NT = (((1,), (1,)), ((), ()))


def _matmul_wt_kernel(*refs, npiece, piece):
    a_ref, w_refs, o_ref, w16_ref = refs[1], refs[2:2 + npiece], refs[2 + npiece], refs[3 + npiece]

    @pl.when(pl.program_id(1) == 0)
    def _():
        wt_tile = jnp.concatenate([w_ref[...] for w_ref in w_refs], axis=0)
        w16_ref[...] = wt_tile.T.astype(BF16)

    o_ref[...] = jnp.dot(a_ref[...], w16_ref[...], preferred_element_type=F32).astype(o_ref.dtype)


def _matmul_wt(a, wt, layer, tile_rows, piece, out_dtype, bm, bn, name):
    m, k = a.shape
    npiece = bn // piece
    n_tiles = len(tile_rows)
    starts = jnp.asarray(tile_rows, jnp.int32) // piece
    assert all(r % piece == 0 for r in tile_rows)

    def piece_spec(t):
        def index(j, i, s):
            nxt = jnp.minimum(j + (i > t).astype(jnp.int32), n_tiles - 1)
            return (layer, s[nxt] + t, 0)
        return pl.BlockSpec((None, piece, k), index)

    return pl.pallas_call(
        functools.partial(_matmul_wt_kernel, npiece=npiece, piece=piece),
        grid_spec=pltpu.PrefetchScalarGridSpec(
            num_scalar_prefetch=1,
            grid=(n_tiles, m // bm),
            in_specs=[pl.BlockSpec((bm, k), lambda j, i, s: (i, 0))] + [piece_spec(t) for t in range(npiece)],
            out_specs=pl.BlockSpec((bm, bn), lambda j, i, s: (i, j)),
            scratch_shapes=[pltpu.VMEM((k, bn), BF16)],
        ),
        out_shape=jax.ShapeDtypeStruct((m, n_tiles * bn), out_dtype),
        compiler_params=_params("parallel", "arbitrary"),
        name=name,
    )(starts, a, *([wt] * npiece))


def _dt_kernel(u_ref, wt_ref, bias_ref, alog_ref, dt_ref, acs_ref, acst_ref):
    raw = lax.dot_general(u_ref[...], wt_ref[...].astype(BF16), _NT, preferred_element_type=F32) + bias_ref[...]
    dt = jnp.maximum(raw, 0.0) + jnp.log1p(jnp.exp(-jnp.abs(raw)))
    a = dt * (-jnp.exp(alog_ref[...]))
    li = lax.broadcasted_iota(jnp.int32, (CHUNK, CHUNK), 0)
    ki = lax.broadcasted_iota(jnp.int32, (CHUNK, CHUNK), 1)
    tril = (ki <= li).astype(F32)
    acs = jnp.dot(tril, a, precision=lax.Precision.HIGHEST, preferred_element_type=F32)
    dt_ref[...] = dt
    acs_ref[...] = acs
    acst_ref[0] = acs.T


def _dt_prep(u, wt, layer, dt_row0, dt_bias, a_log):
    m, d = u.shape
    h = dt_bias.shape[0]
    nc = m // CHUNK
    assert dt_row0 % h == 0
    return pl.pallas_call(
        _dt_kernel,
        grid=(nc,),
        in_specs=[
            pl.BlockSpec((CHUNK, d), lambda i: (i, 0)),
            pl.BlockSpec((None, h, d), lambda i: (layer, dt_row0 // h, 0)),
            pl.BlockSpec((1, h), lambda i: (0, 0)),
            pl.BlockSpec((1, h), lambda i: (0, 0)),
        ],
        out_specs=[
            pl.BlockSpec((CHUNK, h), lambda i: (i, 0)),
            pl.BlockSpec((CHUNK, h), lambda i: (i, 0)),
            pl.BlockSpec((1, h, CHUNK), lambda i: (i, 0, 0)),
        ],
        out_shape=[
            jax.ShapeDtypeStruct((m, h), F32),
            jax.ShapeDtypeStruct((m, h), F32),
            jax.ShapeDtypeStruct((nc, h, CHUNK), F32),
        ],
        compiler_params=_params("parallel"),
        name="dt_prep",
    )(u, wt, dt_bias.reshape(1, h), a_log.reshape(1, h))


def _kr_kernel(u_ref, wt_ref, o_ref):
    w = wt_ref[...]
    half = QK_ROPE // 2
    w_all = jnp.concatenate([w, w[half:], w[:half]], axis=0).astype(BF16)
    o_ref[...] = lax.dot_general(u_ref[...], w_all, _NT, preferred_element_type=F32).astype(o_ref.dtype)


def _rope_tile(t, cs1, cs2):
    return t * cs1 + pltpu.roll(t, LANES // 2, 1) * cs2


def _kr_proj(u, wt, layer, kr_row0, bm):
    m, d = u.shape
    assert kr_row0 % QK_ROPE == 0
    return pl.pallas_call(
        _kr_kernel,
        grid=(m // bm,),
        in_specs=[
            pl.BlockSpec((bm, d), lambda i: (i, 0)),
            pl.BlockSpec((None, QK_ROPE, d), lambda i: (layer, kr_row0 // QK_ROPE, 0)),
        ],
        out_specs=pl.BlockSpec((bm, LANES), lambda i: (i, 0)),
        out_shape=jax.ShapeDtypeStruct((m, LANES), BF16),
        compiler_params=_params("parallel"),
        name="kr_proj",
    )(u, wt)


def _split_expand(v, onehot2_ref):
    hi = v.astype(BF16)
    mid = (v - hi.astype(F32)).astype(BF16)
    return jnp.dot(jnp.concatenate([hi, mid], axis=1), onehot2_ref[...], preferred_element_type=F32)


def _ssd_kernel(xs_ref, b_ref, c_ref, z_ref, dt_ref, acs_ref, acst_ref, cw_ref, cbias_ref, shift_ref,
                rexp_ref, rexp2_ref, eexp2_ref, dskip_ref, nw_ref,
                y_ref, pad_ref, state_ref, ydiag_ref, *, r, cps):
    gw = r * SSM_HEADDIM
    n = SSM_STATE
    halo = PACKED_ROWS
    rows_step = cps * CHUNK

    @pl.when(pl.program_id(2) == 0)
    def _():
        pad_ref[0:halo, :] = jnp.zeros((halo, gw + 2 * n), BF16)
        state_ref[...] = jnp.zeros_like(state_ref)

    pad_ref[halo:halo + rows_step, 0:gw] = xs_ref[...]
    pad_ref[halo:halo + rows_step, gw:gw + n] = b_ref[...]
    pad_ref[halo:halo + rows_step, gw + n:gw + 2 * n] = c_ref[...]
    li = lax.broadcasted_iota(jnp.int32, (CHUNK, CHUNK), 0)
    si = lax.broadcasted_iota(jnp.int32, (CHUNK, CHUNK), 1)
    causal = si <= li
    low_half = si < SSM_HEADDIM

    for sub in range(cps):
        r0 = sub * CHUNK
        window = pad_ref[r0:r0 + halo + CHUNK, :]
        shifted = jnp.dot(shift_ref[...], window, preferred_element_type=F32)
        conv = cbias_ref[0]
        for k in range(CONV_WIDTH - 1):
            conv = conv + shifted[k * CHUNK:(k + 1) * CHUNK] * cw_ref[0, k:k + 1, :]
        conv = conv + window[halo:halo + CHUNK, :].astype(F32) * cw_ref[0, CONV_WIDTH - 1:CONV_WIDTH, :]
        xc = _silu(conv[:, 0:gw])
        bc = _silu(conv[:, gw:gw + n])
        cc = _silu(conv[:, gw + n:gw + 2 * n])

        cb16 = cc.astype(BF16)
        bb16 = bc.astype(BF16)
        cbm = lax.dot_general(cb16, bb16, _NT, preferred_element_type=F32)

        acs = acs_ref[r0:r0 + CHUNK, :]
        acs_last = acs[CHUNK - 1:CHUNK, :]
        per_head = jnp.concatenate([dt_ref[r0:r0 + CHUNK, :], jnp.exp(acs), jnp.exp(acs_last - acs)], axis=0)
        ex = jnp.dot(per_head.astype(BF16), rexp_ref[...], preferred_element_type=F32)
        dt_e = ex[0:CHUNK]
        eacs_e = ex[CHUNK:2 * CHUNK]
        dst_e = ex[2 * CHUNK:3 * CHUNK]
        last_e = _split_expand(jnp.broadcast_to(acs_last, (PACKED_ROWS, acs.shape[1])), rexp2_ref)
        cd_e = jnp.exp(last_e[0:1, :])
        col_all = _split_expand(acs, eexp2_ref)

        x32 = xc * dt_e
        x16 = x32.astype(BF16)
        zero16 = jnp.zeros((CHUNK, LANES), BF16)
        for p in range(r // 2):
            xp = x16[:, p * LANES:(p + 1) * LANES]
            decays = []
            for q in range(2):
                j = 2 * p + q
                seg = col_all[:, j * LANES:(j + 1) * LANES] - acst_ref[sub, j:j + 1, :]
                decays.append((jnp.exp(jnp.where(causal, seg, -jnp.inf)) * cbm).astype(BF16))
            x_pair = jnp.concatenate([jnp.where(low_half, xp, zero16), jnp.where(low_half, zero16, xp)], axis=0)
            ydiag_ref[sub, :, p * LANES:(p + 1) * LANES] = jnp.dot(
                jnp.concatenate(decays, axis=1), x_pair, preferred_element_type=F32)

        st = state_ref[...]
        y_off = jnp.dot(cb16, st.astype(BF16), preferred_element_type=F32) * eacs_e
        xd16 = (x32 * dst_e).astype(BF16)
        state_ref[...] = st * cd_e + jnp.dot(bc.T.astype(BF16), xd16, preferred_element_type=F32)

        y = ydiag_ref[sub] + y_off + xc * dskip_ref[...]
        gv = y * _silu(z_ref[r0:r0 + CHUNK, :].astype(F32))
        ms = jnp.mean(gv * gv, axis=-1, keepdims=True)
        y_ref[r0:r0 + CHUNK, :] = ((gv * lax.rsqrt(ms + EPS)) * nw_ref[...]).astype(y_ref.dtype)

    pad_ref[0:halo, :] = pad_ref[rows_step:rows_step + halo, :]


def _ssd(proj, off, dt, acs, acst, conv_w, conv_b, d_skip, ssm_norm_w, batch, seq, groups, d_ssm, cps):
    m = proj.shape[0]
    heads = dt.shape[1]
    r = heads // groups
    gw = r * SSM_HEADDIM
    n = SSM_STATE
    nc = seq // CHUNK
    assert heads == LANES and r % 2 == 0 and gw % LANES == 0 and seq % (cps * CHUNK) == 0

    head_of_col = jnp.arange(d_ssm, dtype=jnp.int32) // SSM_HEADDIM
    rexp = (jnp.arange(heads, dtype=jnp.int32)[:, None] == head_of_col[None, :]).astype(BF16)
    head_of_tile = jnp.arange(heads * LANES, dtype=jnp.int32) // LANES
    eexp = (jnp.arange(heads, dtype=jnp.int32)[:, None] == head_of_tile[None, :]).astype(BF16)
    rexp2 = jnp.concatenate([rexp] * 2, axis=0)
    eexp2 = jnp.concatenate([eexp] * 2, axis=0)
    dskip_e = jnp.repeat(d_skip.astype(F32), SSM_HEADDIM).reshape(1, d_ssm)

    def per_group(a):
        rows = a.shape[0]
        xs = a[:, :d_ssm].reshape(rows, groups, gw)
        bs = a[:, d_ssm:d_ssm + groups * n].reshape(rows, groups, n)
        cs = a[:, d_ssm + groups * n:].reshape(rows, groups, n)
        return jnp.concatenate([xs, bs, cs], axis=2).transpose(1, 0, 2)

    cw = per_group(conv_w.astype(F32))
    cb = per_group(conv_b.astype(F32).reshape(1, -1))
    halo = PACKED_ROWS
    t_idx = jnp.arange((CONV_WIDTH - 1) * CHUNK, dtype=jnp.int32)
    src = halo + (t_idx % CHUNK) - (CONV_WIDTH - 1) + t_idx // CHUNK
    shift = (src[:, None] == jnp.arange(halo + CHUNK, dtype=jnp.int32)[None, :]).astype(BF16)

    ns = nc // cps
    rows = cps * CHUNK

    def row(b, g, c):
        return b * ns + c

    xs0, b0, c0, z0 = off["xs"] // gw, off["B"] // n, off["C"] // n, off["z"] // gw
    cols = gw + 2 * n
    in_specs = [
        pl.BlockSpec((rows, gw), lambda b, g, c: (row(b, g, c), xs0 + g)),
        pl.BlockSpec((rows, n), lambda b, g, c: (row(b, g, c), b0 + g)),
        pl.BlockSpec((rows, n), lambda b, g, c: (row(b, g, c), c0 + g)),
        pl.BlockSpec((rows, gw), lambda b, g, c: (row(b, g, c), z0 + g)),
        pl.BlockSpec((rows, heads), lambda b, g, c: (row(b, g, c), 0)),
        pl.BlockSpec((rows, heads), lambda b, g, c: (row(b, g, c), 0)),
        pl.BlockSpec((cps, r, CHUNK), lambda b, g, c: (row(b, g, c), g, 0)),
        pl.BlockSpec((1, CONV_WIDTH, cols), lambda b, g, c: (g, 0, 0)),
        pl.BlockSpec((1, 1, cols), lambda b, g, c: (g, 0, 0)),
        pl.BlockSpec(((CONV_WIDTH - 1) * CHUNK, halo + CHUNK), lambda b, g, c: (0, 0)),
        pl.BlockSpec((heads, gw), lambda b, g, c: (0, g)),
        pl.BlockSpec((2 * heads, gw), lambda b, g, c: (0, g)),
        pl.BlockSpec((2 * heads, r * LANES), lambda b, g, c: (0, g)),
        pl.BlockSpec((1, gw), lambda b, g, c: (0, g)),
        pl.BlockSpec((1, gw), lambda b, g, c: (0, g)),
    ]
    return pl.pallas_call(
        functools.partial(_ssd_kernel, r=r, cps=cps),
        grid=(batch, groups, ns),
        in_specs=in_specs,
        out_specs=pl.BlockSpec((rows, gw), lambda b, g, c: (row(b, g, c), g)),
        out_shape=jax.ShapeDtypeStruct((m, d_ssm), BF16),
        scratch_shapes=[
            pltpu.VMEM((halo + rows, cols), BF16),
            pltpu.VMEM((n, gw), F32),
            pltpu.VMEM((cps, CHUNK, gw), F32),
        ],
        compiler_params=_params("parallel", "parallel", "arbitrary"),
        name="ssd_scan",
    )(proj, proj, proj, proj, dt, acs, acst, cw, cb, shift,
      rexp, rexp2, eexp2, dskip_e, ssm_norm_w.astype(F32).reshape(1, d_ssm))


def _normed(x_refs, w_ref):
    x = jnp.concatenate([r[...] for r in x_refs], axis=1).astype(F32)
    ms = jnp.mean(x * x, axis=-1, keepdims=True)
    return ((x * lax.rsqrt(ms + EPS)) * w_ref[...]).astype(BF16)


def _piece_specs(offset, width, bm):
    piece = math.gcd(offset, width)
    assert piece % LANES == 0
    first = offset // piece
    return [pl.BlockSpec((bm, piece), lambda j, i, t=t: (i, first + t)) for t in range(width // piece)]


def _q_kernel(*refs, hb, scale):
    nw_ref, wq_ref, cs1_ref, cs2_ref, o_ref = refs[-5:]
    cqn = _normed(refs[:-5], nw_ref)
    q = jnp.dot(cqn, wq_ref[...], preferred_element_type=F32)
    cs1 = cs1_ref[...] * scale
    cs2 = cs2_ref[...] * scale
    for h in range(hb):
        c0 = h * QK_PAD
        o_ref[:, c0:c0 + LANES] = (q[:, c0:c0 + LANES] * scale).astype(o_ref.dtype)
        o_ref[:, c0 + LANES:c0 + QK_PAD] = _rope_tile(q[:, c0 + LANES:c0 + QK_PAD], cs1, cs2).astype(o_ref.dtype)


def _kv_kernel(*refs, hb):
    kr_ref, nw_ref, w_ref, cs1_ref, cs2_ref, k_ref, v_ref = refs[-7:]
    ckv = _normed(refs[:-7], nw_ref)
    kv = jnp.dot(ckv, w_ref[...], preferred_element_type=F32)
    roped = _rope_tile(kr_ref[...].astype(F32), cs1_ref[...], cs2_ref[...]).astype(k_ref.dtype)
    lane = lax.broadcasted_iota(jnp.int32, roped.shape, 1)
    ones_col = jnp.where(lane == 0, 1.0, 0.0).astype(v_ref.dtype)
    for h in range(hb):
        c0 = h * QK_PAD
        k_ref[:, c0:c0 + LANES] = kv[:, c0:c0 + LANES].astype(k_ref.dtype)
        k_ref[:, c0 + LANES:c0 + QK_PAD] = roped
        v_ref[:, c0:c0 + V_HEAD] = kv[:, c0 + LANES:c0 + QK_PAD].astype(v_ref.dtype)
        v_ref[:, c0 + V_HEAD:c0 + V_PAD] = ones_col


def _rope_tables(seq):
    half = QK_ROPE // 2
    inv_freq = ROPE_THETA ** (-jnp.arange(0, half, dtype=F32) / half)
    ang = jnp.arange(seq, dtype=jnp.int32).astype(F32)[:, None] * inv_freq[None, :]
    cos, sin = jnp.cos(ang), jnp.sin(ang)
    zeros = jnp.zeros((seq, LANES - QK_ROPE), F32)
    return jnp.concatenate([cos, cos, zeros], axis=1), jnp.concatenate([-sin, sin, zeros], axis=1)


def _q_proj(proj, cq_off, q_norm_w, w_uq, cs1, cs2, seq, heads, bm, hb):
    m = proj.shape[0]
    ql = w_uq.shape[0]
    w = w_uq.reshape(ql, heads, QK_NOPE + QK_ROPE)
    nope, rp = w[:, :, :QK_NOPE], w[:, :, QK_NOPE:]
    half = QK_ROPE // 2
    wq = jnp.concatenate([nope, rp, rp[:, :, half:], rp[:, :, :half]], axis=2).reshape(ql, heads * QK_PAD).astype(BF16)
    nb = seq // bm
    scale = LOG2E / math.sqrt(QK_NOPE + QK_ROPE)
    cq_specs = _piece_specs(cq_off, ql, bm)
    return pl.pallas_call(
        functools.partial(_q_kernel, hb=hb, scale=scale),
        grid=(heads // hb, m // bm),
        in_specs=cq_specs + [
            pl.BlockSpec((1, ql), lambda j, i: (0, 0)),
            pl.BlockSpec((ql, hb * QK_PAD), lambda j, i: (0, j)),
            pl.BlockSpec((bm, LANES), lambda j, i: (i % nb, 0)),
            pl.BlockSpec((bm, LANES), lambda j, i: (i % nb, 0)),
        ],
        out_specs=pl.BlockSpec((bm, hb * QK_PAD), lambda j, i: (i, j)),
        out_shape=jax.ShapeDtypeStruct((m, heads * QK_PAD), BF16),
        compiler_params=_params("parallel", "parallel"),
        name="q_proj",
    )(*([proj] * len(cq_specs)), q_norm_w.astype(F32).reshape(1, ql), wq, cs1, cs2)


def _kv_proj(proj, kvc_off, kr, kv_norm_w, w_ukv, cs1, cs2, seq, heads, bm, hb):
    m = proj.shape[0]
    kvl = w_ukv.shape[0]
    nb = seq // bm
    kvc_specs = _piece_specs(kvc_off, kvl, bm)
    return pl.pallas_call(
        functools.partial(_kv_kernel, hb=hb),
        grid=(heads // hb, m // bm),
        in_specs=kvc_specs + [
            pl.BlockSpec((bm, LANES), lambda j, i: (i, 0)),
            pl.BlockSpec((1, kvl), lambda j, i: (0, 0)),
            pl.BlockSpec((kvl, hb * QK_PAD), lambda j, i: (0, j)),
            pl.BlockSpec((bm, LANES), lambda j, i: (i % nb, 0)),
            pl.BlockSpec((bm, LANES), lambda j, i: (i % nb, 0)),
        ],
        out_specs=[
            pl.BlockSpec((bm, hb * QK_PAD), lambda j, i: (i, j)),
            pl.BlockSpec((bm, hb * V_PAD), lambda j, i: (i, j)),
        ],
        out_shape=[
            jax.ShapeDtypeStruct((m, heads * QK_PAD), BF16),
            jax.ShapeDtypeStruct((m, heads * V_PAD), BF16),
        ],
        compiler_params=_params("parallel", "parallel"),
        name="kv_proj",
    )(*([proj] * len(kvc_specs)), kr, kv_norm_w.astype(F32).reshape(1, kvl),
      w_ukv.astype(BF16), cs1, cs2)


def _attn_kernel(q_ref, k_ref, v_ref, g_ref, o_ref, m_ref, acc_ref, s_ref, *, bq, bk, hpb):
    qi = pl.program_id(2)
    m_ref[...] = jnp.full(m_ref.shape, MASK_VALUE, F32)
    acc_ref[...] = jnp.zeros_like(acc_ref)

    def scores(j, slot, r0):
        start = pl.multiple_of(j * bk, bk)
        for h in range(hpb):
            s_ref[slot, h, r0:bq, :] = lax.dot_general(
                q_ref[r0:bq, h * QK_PAD:(h + 1) * QK_PAD], k_ref[pl.ds(start, bk), h * QK_PAD:(h + 1) * QK_PAD],
                _NT, preferred_element_type=F32)

    def accumulate(j, slot, r0, masked):
        start = pl.multiple_of(j * bk, bk)
        rows = bq - r0
        for h in range(hpb):
            s = s_ref[slot, h, r0:bq, :]
            v = v_ref[pl.ds(start, bk), h * V_PAD:(h + 1) * V_PAD]
            if masked:
                ri = lax.broadcasted_iota(jnp.int32, (rows, bk), 0)
                ci = lax.broadcasted_iota(jnp.int32, (rows, bk), 1)
                s = jnp.where(ci <= ri, s, MASK_VALUE)
            m_prev = m_ref[h, r0:bq, :]
            m_next = jnp.maximum(m_prev, jnp.max(s, axis=1, keepdims=True))
            p = jnp.exp2(s - jnp.concatenate([m_next] * (bk // LANES), axis=1))
            alpha = jnp.exp2(m_prev - m_next)
            pv = jnp.dot(p.astype(v.dtype), v, preferred_element_type=F32)
            acc_ref[h, r0:bq, :] = jnp.concatenate([alpha] * (V_PAD // LANES), axis=1) * acc_ref[h, r0:bq, :] + pv
            m_ref[h, r0:bq, :] = m_next

    scores(0, 0, 0)

    def body(i, carry):
        j = 2 * i
        scores(j + 1, 1, 0)
        accumulate(j, 0, 0, False)
        scores(j + 2, 0, 0)
        accumulate(j + 1, 1, 0, False)
        return carry

    nsub = bq // bk
    lax.fori_loop(0, qi * (nsub // 2), body, 0)
    for t in range(nsub):
        if t + 1 < nsub:
            scores(qi * nsub + t + 1, (t + 1) % 2, (t + 1) * bk)
        accumulate(qi * nsub + t, t % 2, t * bk, True)

    for h in range(hpb):
        acc = acc_ref[h]
        g = g_ref[:, h * V_HEAD:(h + 1) * V_HEAD].astype(F32)
        o = acc[:, 0:V_HEAD] / acc[:, V_HEAD:V_HEAD + 1]
        o_ref[:, h * V_HEAD:(h + 1) * V_HEAD] = (o * _silu(g)).astype(o_ref.dtype)


def _attention(q, k, v, proj, g_off, batch, seq, heads, bq, bk, hpb):
    m = q.shape[0]
    nq = seq // bq
    assert seq % bq == 0 and bq % (2 * bk) == 0 and g_off % (hpb * V_HEAD) == 0
    g0 = g_off // (hpb * V_HEAD)
    return pl.pallas_call(
        functools.partial(_attn_kernel, bq=bq, bk=bk, hpb=hpb),
        grid=(batch, heads // hpb, nq),
        in_specs=[
            pl.BlockSpec((bq, hpb * QK_PAD), lambda b, h, i: (b * nq + i, h)),
            pl.BlockSpec((seq, hpb * QK_PAD), lambda b, h, i: (b, h)),
            pl.BlockSpec((seq, hpb * V_PAD), lambda b, h, i: (b, h)),
            pl.BlockSpec((bq, hpb * V_HEAD), lambda b, h, i: (b * nq + i, g0 + h)),
        ],
        out_specs=pl.BlockSpec((bq, hpb * V_HEAD), lambda b, h, i: (b * nq + i, h)),
        out_shape=jax.ShapeDtypeStruct((m, heads * V_HEAD), BF16),
        scratch_shapes=[
            pltpu.VMEM((hpb, bq, LANES), F32),
            pltpu.VMEM((hpb, bq, V_PAD), F32),
            pltpu.VMEM((2, hpb, bq, bk), F32),
        ],
        compiler_params=_params("parallel", "parallel", "arbitrary"),
        name="mla_attention",
    )(q, k, v, proj)


def _merge_kernel(ys_ref, ya_ref, ws_ref, wa_ref, gs_ref, ga_ref, o_ref):
    a = jnp.dot(ys_ref[...], ws_ref[...], preferred_element_type=F32)
    b = jnp.dot(ya_ref[...], wa_ref[...], preferred_element_type=F32)
    o = _sigmoid(gs_ref[...].astype(F32)) * a + _sigmoid(ga_ref[...].astype(F32)) * b
    o_ref[...] = o.astype(o_ref.dtype)


def _merge(y_ssm, y_attn, w_s, w_a, proj, off, bm, bn):
    m, ds = y_ssm.shape
    da = y_attn.shape[1]
    d = w_s.shape[1]
    gs0, ga0 = off["gate_ssm"] // bn, off["gate_attn"] // bn
    return pl.pallas_call(
        _merge_kernel,
        grid=(d // bn, m // bm),
        in_specs=[
            pl.BlockSpec((bm, ds), lambda j, i: (i, 0)),
            pl.BlockSpec((bm, da), lambda j, i: (i, 0)),
            pl.BlockSpec((ds, bn), lambda j, i: (0, j)),
            pl.BlockSpec((da, bn), lambda j, i: (0, j)),
            pl.BlockSpec((bm, bn), lambda j, i: (i, gs0 + j)),
            pl.BlockSpec((bm, bn), lambda j, i: (i, ga0 + j)),
        ],
        out_specs=pl.BlockSpec((bm, bn), lambda j, i: (i, j)),
        out_shape=jax.ShapeDtypeStruct((m, d), BF16),
        compiler_params=_params("parallel", "parallel"),
        name="merge_proj",
    )(y_ssm, y_attn, w_s, w_a, proj, proj)


def _out_kernel(a_ref, b_ref, x_ref, nw_ref, o_ref, *, nj, bn, final_norm):
    j = pl.program_id(1)
    hblk = x_ref[...] + jnp.dot(a_ref[...], b_ref[...], preferred_element_type=F32)
    for jj in range(nj):
        @pl.when(j == jj)
        def _(jj=jj):
            o_ref[:, jj * bn:(jj + 1) * bn] = hblk

    if final_norm:
        @pl.when(j == nj - 1)
        def _():
            hf = o_ref[...]
            ms = jnp.mean(hf * hf, axis=-1, keepdims=True)
            o_ref[...] = (hf * lax.rsqrt(ms + EPS)) * nw_ref[...]


def _out_proj(merged, w_out, x, norm_w, final_norm, bm, bn):
    m, k = merged.shape
    n = w_out.shape[1]
    nj = n // bn
    return pl.pallas_call(
        functools.partial(_out_kernel, nj=nj, bn=bn, final_norm=final_norm),
        grid=(m // bm, nj),
        in_specs=[
            pl.BlockSpec((bm, k), lambda i, j: (i, 0)),
            pl.BlockSpec((k, bn), lambda i, j: (0, j)),
            pl.BlockSpec((bm, bn), lambda i, j: (i, j)),
            pl.BlockSpec((1, n), lambda i, j: (0, 0)),
        ],
        out_specs=pl.BlockSpec((bm, n), lambda i, j: (i, 0)),
        out_shape=jax.ShapeDtypeStruct((m, n), F32),
        compiler_params=_params("parallel", "arbitrary"),
        name="out_proj",
    )(merged, w_out, x, norm_w.astype(F32).reshape(1, n))


def _in_proj_layout(n_cols, d_ssm, groups, heads_ssm, q_lora, kv_lora, d_attn, d_model, bn_in):
    gn = groups * SSM_STATE
    sizes = (d_ssm, d_ssm + 2 * gn, heads_ssm, q_lora, kv_lora + QK_ROPE, d_attn, d_model, d_model)
    starts = [0]
    for s in sizes[:-1]:
        starts.append(starts[-1] + s)
    z0, xbc0, dt0, cq0, kv0, ga0, gs0, gt0 = starts
    n_head = -(-(kv0 + kv_lora) // bn_in)
    n_gate = (d_attn + 2 * d_model) // bn_in
    assert (d_attn + 2 * d_model) % bn_in == 0 and n_head * bn_in <= n_cols and gt0 + d_model == n_cols
    tile_cols = [t * bn_in for t in range(n_head)] + [ga0 + t * bn_in for t in range(n_gate)]
    gates0 = n_head * bn_in
    off = {"z": z0, "xs": xbc0, "B": xbc0 + d_ssm, "C": xbc0 + d_ssm + gn, "cq": cq0, "kvc": kv0,
           "g_attn": gates0, "gate_ssm": gates0 + d_attn, "gate_attn": gates0 + d_attn + d_model}
    return tile_cols, off, dt0, kv0 + kv_lora


def kernel(x, norm_in_w, w_in, conv_w, conv_b, dt_bias, a_log, d_skip, ssm_norm_w, q_norm_w, w_uq,
           kv_norm_w, w_ukv, w_branch_ssm, w_branch_attn, w_out, norm_final_w):
    batch, seq, d_model = x.shape
    depth = w_in.shape[0]
    m = batch * seq
    d_ssm = w_branch_ssm.shape[1]
    d_attn = w_branch_attn.shape[1]
    heads_ssm = dt_bias.shape[1]
    groups = (conv_w.shape[2] - d_ssm) // (2 * SSM_STATE)
    gw = (heads_ssm // groups) * SSM_HEADDIM
    q_lora, kv_lora = w_uq.shape[1], w_ukv.shape[1]
    heads = d_attn // V_HEAD
    assert heads_ssm * SSM_HEADDIM == d_ssm and w_uq.shape[2] == heads * (QK_NOPE + QK_ROPE)

    bm_norm = min(256, m)
    bm_in, bn_in = min(1024, m), 512
    cps_ssd = 4
    bm_qkv, hb_qkv = min(512, seq), 8
    bq_attn, bk_attn, hpb_attn = min(2048, seq), min(512, seq // 2), 1
    bm_merge, bn_merge = min(512, m), 512
    bm_out, bn_out = min(512, m), 512

    cs1, cs2 = _rope_tables(seq)
    wt_in = jnp.swapaxes(w_in, 1, 2)
    tile_cols, off, dt_col, kr_col = _in_proj_layout(w_in.shape[2], d_ssm, groups, heads_ssm, q_lora, kv_lora,
                                                     d_attn, d_model, bn_in)
    piece = math.gcd(bn_in, *tile_cols[1:])
    h = x.reshape(m, d_model)
    for layer in range(depth):
        u = _rmsnorm(h, norm_in_w[layer], BF16, bm_norm, "norm_in")
        proj = _matmul_wt(u, wt_in, layer, tile_cols, piece, BF16, bm_in, bn_in, "in_proj")
        kr = _kr_proj(u, wt_in, layer, kr_col, bm_in)

        dt, acs, acst = _dt_prep(u, wt_in, layer, dt_col, dt_bias[layer].astype(F32), a_log[layer].astype(F32))
        y_ssm = _ssd(proj, off, dt, acs, acst, conv_w[layer], conv_b[layer], d_skip[layer],
                     ssm_norm_w[layer], batch, seq, groups, d_ssm, cps_ssd)

        q = _q_proj(proj, off["cq"], q_norm_w[layer], w_uq[layer], cs1, cs2, seq, heads, bm_qkv, hb_qkv)
        k, v = _kv_proj(proj, off["kvc"], kr, kv_norm_w[layer], w_ukv[layer], cs1, cs2, seq, heads, bm_qkv, hb_qkv)
        y_attn = _attention(q, k, v, proj, off["g_attn"], batch, seq, heads, bq_attn, bk_attn, hpb_attn)

        merged = _merge(y_ssm, y_attn, w_branch_ssm[layer].astype(BF16), w_branch_attn[layer].astype(BF16),
                        proj, off, bm_merge, bn_merge)
        last = layer == depth - 1
        h = _out_proj(merged, w_out[layer].astype(BF16), h, norm_final_w, last, bm_out, bn_out)
    return h.reshape(batch, seq, d_model)
```

```python
import functools
import math

import jax
import jax.numpy as jnp
from jax import lax
from jax.experimental import pallas as pl
from jax.experimental.pallas import tpu as pltpu

F32 = jnp.float32
BF16 = jnp.bfloat16

EPS = 1e-6
SSM_HEADDIM = 64
SSM_STATE = 128
CONV_WIDTH = 4
CHUNK = 128
QK_NOPE = 128
QK_ROPE = 64
V_HEAD = 128
ROPE_THETA = 10000.0

LANES = 128
SUBLANES = 8
PACKED_ROWS = 16
VMEM_LIMIT_BYTES = 56 * 1024 * 1024
QK_PAD = 2 * LANES
V_PAD = 2 * LANES
MASK_VALUE = -1e30
LOG2E = math.log2(math.e)


def _params(*sem):
    return pltpu.CompilerParams(dimension_semantics=sem, vmem_limit_bytes=VMEM_LIMIT_BYTES)


def _sigmoid(x):
    return 0.5 + 0.5 * jnp.tanh(0.5 * x)


def _silu(x):
    h = 0.5 * x
    return h + h * jnp.tanh(h)


def _rmsnorm_kernel(x_ref, w_ref, o_ref):
    x = x_ref[...].astype(F32)
    ms = jnp.mean(x * x, axis=-1, keepdims=True)
    o_ref[...] = ((x * lax.rsqrt(ms + EPS)) * w_ref[...]).astype(o_ref.dtype)


def _rmsnorm(x, w, out_dtype, bm, name):
    m, d = x.shape
    return pl.pallas_call(
        _rmsnorm_kernel,
        grid=(m // bm,),
        in_specs=[pl.BlockSpec((bm, d), lambda i: (i, 0)), pl.BlockSpec((1, d), lambda i: (0, 0))],
        out_specs=pl.BlockSpec((bm, d), lambda i: (i, 0)),
        out_shape=jax.ShapeDtypeStruct((m, d), out_dtype),
        compiler_params=_params("parallel"),
        name=name,
    )(x, w.reshape(1, d))


def _matmul_kernel(a_ref, b_ref, o_ref):
    o_ref[...] = jnp.dot(a_ref[...], b_ref[...], preferred_element_type=F32).astype(o_ref.dtype)


def _matmul(a, b, out_dtype, bm, bn, name):
    m, k = a.shape
    n = b.shape[1]
    return pl.pallas_call(
        _matmul_kernel,
        grid=(n // bn, m // bm),
        in_specs=[pl.BlockSpec((bm, k), lambda j, i: (i, 0)), pl.BlockSpec((k, bn), lambda j, i: (0, j))],
        out_specs=pl.BlockSpec((bm, bn), lambda j, i: (i, j)),
        out_shape=jax.ShapeDtypeStruct((m, n), out_dtype),
        compiler_params=_params("parallel", "parallel"),
        name=name,
    )(a, b)


_NT = (((1,), (1,)), ((), ()))


def _matmul_wt_kernel(*refs, pps, piece, n_tiles):
    a_ref, w_refs, o_ref, w16_ref = refs[1], refs[2:2 + pps], refs[2 + pps], refs[3 + pps]
    j = pl.program_id(0)
    i = pl.program_id(1)
    npiece = w16_ref.shape[1] // piece

    @pl.when(j < n_tiles)
    def _():
        for p, w_ref in enumerate(w_refs):
            t = i * pps + p

            @pl.when(t < npiece)
            def _(w_ref=w_ref, t=t):
                w16_ref[j % 2, pl.ds(pl.multiple_of(t * piece, piece), piece), :] = w_ref[...].astype(BF16)

    @pl.when(j == 0)
    def _():
        o_ref[...] = jnp.zeros_like(o_ref)

    @pl.when(j > 0)
    def _():
        o_ref[...] = lax.dot_general(a_ref[...], w16_ref[(j - 1) % 2], _NT,
                                     preferred_element_type=F32).astype(o_ref.dtype)


def _matmul_wt(a, wt, layer, tile_rows, piece, out_dtype, bm, bn, name):
    m, k = a.shape
    npiece = bn // piece
    n_tiles = len(tile_rows)
    n_row = m // bm
    pps = -(-npiece // n_row)
    starts = jnp.asarray(tile_rows, jnp.int32) // piece
    assert all(r % piece == 0 for r in tile_rows)

    def piece_spec(p):
        def index(j, i, s):
            return (layer, s[jnp.minimum(j, n_tiles - 1)] + jnp.minimum(i * pps + p, npiece - 1), 0)
        return pl.BlockSpec((None, piece, k), index)

    return pl.pallas_call(
        functools.partial(_matmul_wt_kernel, pps=pps, piece=piece, n_tiles=n_tiles),
        grid_spec=pltpu.PrefetchScalarGridSpec(
            num_scalar_prefetch=1,
            grid=(n_tiles + 1, n_row),
            in_specs=[pl.BlockSpec((bm, k), lambda j, i, s: (i, 0))] + [piece_spec(p) for p in range(pps)],
            out_specs=pl.BlockSpec((bm, bn), lambda j, i, s: (i, j)),
            scratch_shapes=[pltpu.VMEM((2, bn, k), BF16)],
        ),
        out_shape=jax.ShapeDtypeStruct((m, (n_tiles + 1) * bn), out_dtype),
        compiler_params=_params("arbitrary", "arbitrary"),
        name=name,
    )(starts, a, *([wt] * pps))


def _dt_kernel(u_ref, wt_ref, bias_ref, alog_ref, dt_ref, acs_ref, acst_ref):
    raw = lax.dot_general(u_ref[...], wt_ref[...].astype(BF16), _NT, preferred_element_type=F32) + bias_ref[...]
    dt = jnp.maximum(raw, 0.0) + jnp.log1p(jnp.exp(-jnp.abs(raw)))
    a = dt * (-jnp.exp(alog_ref[...]))
    li = lax.broadcasted_iota(jnp.int32, (CHUNK, CHUNK), 0)
    ki = lax.broadcasted_iota(jnp.int32, (CHUNK, CHUNK), 1)
    tril = (ki <= li).astype(F32)
    acs = jnp.dot(tril, a, precision=lax.Precision.HIGHEST, preferred_element_type=F32)
    dt_ref[...] = dt
    acs_ref[...] = acs
    acst_ref[0] = acs.T


def _dt_prep(u, wt, layer, dt_row0, dt_bias, a_log):
    m, d = u.shape
    h = dt_bias.shape[0]
    nc = m // CHUNK
    assert dt_row0 % h == 0
    return pl.pallas_call(
        _dt_kernel,
        grid=(nc,),
        in_specs=[
            pl.BlockSpec((CHUNK, d), lambda i: (i, 0)),
            pl.BlockSpec((None, h, d), lambda i: (layer, dt_row0 // h, 0)),
            pl.BlockSpec((1, h), lambda i: (0, 0)),
            pl.BlockSpec((1, h), lambda i: (0, 0)),
        ],
        out_specs=[
            pl.BlockSpec((CHUNK, h), lambda i: (i, 0)),
            pl.BlockSpec((CHUNK, h), lambda i: (i, 0)),
            pl.BlockSpec((1, h, CHUNK), lambda i: (i, 0, 0)),
        ],
        out_shape=[
            jax.ShapeDtypeStruct((m, h), F32),
            jax.ShapeDtypeStruct((m, h), F32),
            jax.ShapeDtypeStruct((nc, h, CHUNK), F32),
        ],
        compiler_params=_params("parallel"),
        name="dt_prep",
    )(u, wt, dt_bias.reshape(1, h), a_log.reshape(1, h))


def _kr_kernel(u_ref, wt_ref, o_ref):
    w = wt_ref[...]
    half = QK_ROPE // 2
    w_all = jnp.concatenate([w, w[half:], w[:half]], axis=0).astype(BF16)
    o_ref[...] = lax.dot_general(u_ref[...], w_all, _NT, preferred_element_type=F32).astype(o_ref.dtype)


def _rope_tile(t, cs1, cs2):
    return t * cs1 + pltpu.roll(t, LANES // 2, 1) * cs2


def _kr_proj(u, wt, layer, kr_row0, bm):
    m, d = u.shape
    assert kr_row0 % QK_ROPE == 0
    return pl.pallas_call(
        _kr_kernel,
        grid=(m // bm,),
        in_specs=[
            pl.BlockSpec((bm, d), lambda i: (i, 0)),
            pl.BlockSpec((None, QK_ROPE, d), lambda i: (layer, kr_row0 // QK_ROPE, 0)),
        ],
        out_specs=pl.BlockSpec((bm, LANES), lambda i: (i, 0)),
        out_shape=jax.ShapeDtypeStruct((m, LANES), BF16),
        compiler_params=_params("parallel"),
        name="kr_proj",
    )(u, wt)


def _split_expand(v, onehot2_ref):
    hi = v.astype(BF16)
    mid = (v - hi.astype(F32)).astype(BF16)
    return jnp.dot(jnp.concatenate([hi, mid], axis=1), onehot2_ref[...], preferred_element_type=F32)


def _ssd_kernel(xs_ref, b_ref, c_ref, z_ref, dt_ref, acs_ref, acst_ref, cw_ref, cbias_ref, shift_ref,
                rexp_ref, rexp2_ref, eexp2_ref, dskip_ref, nw_ref,
                y_ref, pad_ref, state_ref, ydiag_ref, *, r, cps):
    gw = r * SSM_HEADDIM
    n = SSM_STATE
    halo = PACKED_ROWS
    rows_step = cps * CHUNK

    @pl.when(pl.program_id(2) == 0)
    def _():
        pad_ref[0:halo, :] = jnp.zeros((halo, gw + 2 * n), BF16)
        state_ref[...] = jnp.zeros_like(state_ref)

    pad_ref[halo:halo + rows_step, 0:gw] = xs_ref[...]
    pad_ref[halo:halo + rows_step, gw:gw + n] = b_ref[...]
    pad_ref[halo:halo + rows_step, gw + n:gw + 2 * n] = c_ref[...]
    li = lax.broadcasted_iota(jnp.int32, (CHUNK, CHUNK), 0)
    si = lax.broadcasted_iota(jnp.int32, (CHUNK, CHUNK), 1)
    causal = si <= li
    low_half = si < SSM_HEADDIM

    for sub in range(cps):
        r0 = sub * CHUNK
        window = pad_ref[r0:r0 + halo + CHUNK, :]
        shifted = jnp.dot(shift_ref[...], window, preferred_element_type=F32)
        conv = cbias_ref[0]
        for k in range(CONV_WIDTH - 1):
            conv = conv + shifted[k * CHUNK:(k + 1) * CHUNK] * cw_ref[0, k:k + 1, :]
        conv = conv + window[halo:halo + CHUNK, :].astype(F32) * cw_ref[0, CONV_WIDTH - 1:CONV_WIDTH, :]
        xc = _silu(conv[:, 0:gw])
        bc = _silu(conv[:, gw:gw + n])
        cc = _silu(conv[:, gw + n:gw + 2 * n])

        cb16 = cc.astype(BF16)
        bb16 = bc.astype(BF16)
        cbm = lax.dot_general(cb16, bb16, _NT, preferred_element_type=F32)

        acs = acs_ref[r0:r0 + CHUNK, :]
        acs_last = acs[CHUNK - 1:CHUNK, :]
        per_head = jnp.concatenate([dt_ref[r0:r0 + CHUNK, :], jnp.exp(acs), jnp.exp(acs_last - acs)], axis=0)
        ex = jnp.dot(per_head.astype(BF16), rexp_ref[...], preferred_element_type=F32)
        dt_e = ex[0:CHUNK]
        eacs_e = ex[CHUNK:2 * CHUNK]
        dst_e = ex[2 * CHUNK:3 * CHUNK]
        last_e = _split_expand(jnp.broadcast_to(acs_last, (PACKED_ROWS, acs.shape[1])), rexp2_ref)
        cd_e = jnp.exp(last_e[0:1, :])
        col_all = _split_expand(acs, eexp2_ref)

        x32 = xc * dt_e
        x16 = x32.astype(BF16)
        zero16 = jnp.zeros((CHUNK, LANES), BF16)
        for p in range(r // 2):
            xp = x16[:, p * LANES:(p + 1) * LANES]
            decays = []
            for q in range(2):
                j = 2 * p + q
                seg = col_all[:, j * LANES:(j + 1) * LANES] - acst_ref[sub, j:j + 1, :]
                decays.append((jnp.exp(jnp.where(causal, seg, -jnp.inf)) * cbm).astype(BF16))
            x_pair = jnp.concatenate([jnp.where(low_half, xp, zero16), jnp.where(low_half, zero16, xp)], axis=0)
            ydiag_ref[sub, :, p * LANES:(p + 1) * LANES] = jnp.dot(
                jnp.concatenate(decays, axis=1), x_pair, preferred_element_type=F32)

        st = state_ref[...]
        y_off = jnp.dot(cb16, st.astype(BF16), preferred_element_type=F32) * eacs_e
        xd16 = (x32 * dst_e).astype(BF16)
        state_ref[...] = st * cd_e + jnp.dot(bc.T.astype(BF16), xd16, preferred_element_type=F32)

        y = ydiag_ref[sub] + y_off + xc * dskip_ref[...]
        gv = y * _silu(z_ref[r0:r0 + CHUNK, :].astype(F32))
        ms = jnp.mean(gv * gv, axis=-1, keepdims=True)
        y_ref[r0:r0 + CHUNK, :] = ((gv * lax.rsqrt(ms + EPS)) * nw_ref[...]).astype(y_ref.dtype)

    pad_ref[0:halo, :] = pad_ref[rows_step:rows_step + halo, :]


def _ssd(proj, off, dt, acs, acst, conv_w, conv_b, d_skip, ssm_norm_w, batch, seq, groups, d_ssm, cps):
    m = proj.shape[0]
    heads = dt.shape[1]
    r = heads // groups
    gw = r * SSM_HEADDIM
    n = SSM_STATE
    nc = seq // CHUNK
    assert heads == LANES and r % 2 == 0 and gw % LANES == 0 and seq % (cps * CHUNK) == 0

    head_of_col = jnp.arange(d_ssm, dtype=jnp.int32) // SSM_HEADDIM
    rexp = (jnp.arange(heads, dtype=jnp.int32)[:, None] == head_of_col[None, :]).astype(BF16)
    head_of_tile = jnp.arange(heads * LANES, dtype=jnp.int32) // LANES
    eexp = (jnp.arange(heads, dtype=jnp.int32)[:, None] == head_of_tile[None, :]).astype(BF16)
    rexp2 = jnp.concatenate([rexp] * 2, axis=0)
    eexp2 = jnp.concatenate([eexp] * 2, axis=0)
    dskip_e = jnp.repeat(d_skip.astype(F32), SSM_HEADDIM).reshape(1, d_ssm)

    def per_group(a):
        rows = a.shape[0]
        xs = a[:, :d_ssm].reshape(rows, groups, gw)
        bs = a[:, d_ssm:d_ssm + groups * n].reshape(rows, groups, n)
        cs = a[:, d_ssm + groups * n:].reshape(rows, groups, n)
        return jnp.concatenate([xs, bs, cs], axis=2).transpose(1, 0, 2)

    cw = per_group(conv_w.astype(F32))
    cb = per_group(conv_b.astype(F32).reshape(1, -1))
    halo = PACKED_ROWS
    t_idx = jnp.arange((CONV_WIDTH - 1) * CHUNK, dtype=jnp.int32)
    src = halo + (t_idx % CHUNK) - (CONV_WIDTH - 1) + t_idx // CHUNK
    shift = (src[:, None] == jnp.arange(halo + CHUNK, dtype=jnp.int32)[None, :]).astype(BF16)

    ns = nc // cps
    rows = cps * CHUNK

    def row(b, g, c):
        return b * ns + c

    xs0, b0, c0, z0 = off["xs"] // gw, off["B"] // n, off["C"] // n, off["z"] // gw
    cols = gw + 2 * n
    in_specs = [
        pl.BlockSpec((rows, gw), lambda b, g, c: (row(b, g, c), xs0 + g)),
        pl.BlockSpec((rows, n), lambda b, g, c: (row(b, g, c), b0 + g)),
        pl.BlockSpec((rows, n), lambda b, g, c: (row(b, g, c), c0 + g)),
        pl.BlockSpec((rows, gw), lambda b, g, c: (row(b, g, c), z0 + g)),
        pl.BlockSpec((rows, heads), lambda b, g, c: (row(b, g, c), 0)),
        pl.BlockSpec((rows, heads), lambda b, g, c: (row(b, g, c), 0)),
        pl.BlockSpec((cps, r, CHUNK), lambda b, g, c: (row(b, g, c), g, 0)),
        pl.BlockSpec((1, CONV_WIDTH, cols), lambda b, g, c: (g, 0, 0)),
        pl.BlockSpec((1, 1, cols), lambda b, g, c: (g, 0, 0)),
        pl.BlockSpec(((CONV_WIDTH - 1) * CHUNK, halo + CHUNK), lambda b, g, c: (0, 0)),
        pl.BlockSpec((heads, gw), lambda b, g, c: (0, g)),
        pl.BlockSpec((2 * heads, gw), lambda b, g, c: (0, g)),
        pl.BlockSpec((2 * heads, r * LANES), lambda b, g, c: (0, g)),
        pl.BlockSpec((1, gw), lambda b, g, c: (0, g)),
        pl.BlockSpec((1, gw), lambda b, g, c: (0, g)),
    ]
    return pl.pallas_call(
        functools.partial(_ssd_kernel, r=r, cps=cps),
        grid=(batch, groups, ns),
        in_specs=in_specs,
        out_specs=pl.BlockSpec((rows, gw), lambda b, g, c: (row(b, g, c), g)),
        out_shape=jax.ShapeDtypeStruct((m, d_ssm), BF16),
        scratch_shapes=[
            pltpu.VMEM((halo + rows, cols), BF16),
            pltpu.VMEM((n, gw), F32),
            pltpu.VMEM((cps, CHUNK, gw), F32),
        ],
        compiler_params=_params("parallel", "parallel", "arbitrary"),
        name="ssd_scan",
    )(proj, proj, proj, proj, dt, acs, acst, cw, cb, shift,
      rexp, rexp2, eexp2, dskip_e, ssm_norm_w.astype(F32).reshape(1, d_ssm))


def _normed(x_refs, w_ref):
    x = jnp.concatenate([r[...] for r in x_refs], axis=1).astype(F32)
    ms = jnp.mean(x * x, axis=-1, keepdims=True)
    return ((x * lax.rsqrt(ms + EPS)) * w_ref[...]).astype(BF16)


def _piece_specs(offset, width, bm):
    piece = math.gcd(offset, width)
    assert piece % LANES == 0
    first = offset // piece
    return [pl.BlockSpec((bm, piece), lambda j, i, t=t: (i, first + t)) for t in range(width // piece)]


def _q_kernel(*refs, hb, scale):
    nw_ref, wq_ref, cs1_ref, cs2_ref, o_ref = refs[-5:]
    cqn = _normed(refs[:-5], nw_ref)
    q = jnp.dot(cqn, wq_ref[...], preferred_element_type=F32)
    cs1 = cs1_ref[...] * scale
    cs2 = cs2_ref[...] * scale
    for h in range(hb):
        c0 = h * QK_PAD
        o_ref[:, c0:c0 + LANES] = (q[:, c0:c0 + LANES] * scale).astype(o_ref.dtype)
        o_ref[:, c0 + LANES:c0 + QK_PAD] = _rope_tile(q[:, c0 + LANES:c0 + QK_PAD], cs1, cs2).astype(o_ref.dtype)


def _kv_kernel(*refs, hb):
    kr_ref, nw_ref, w_ref, cs1_ref, cs2_ref, k_ref, v_ref = refs[-7:]
    ckv = _normed(refs[:-7], nw_ref)
    kv = jnp.dot(ckv, w_ref[...], preferred_element_type=F32)
    roped = _rope_tile(kr_ref[...].astype(F32), cs1_ref[...], cs2_ref[...]).astype(k_ref.dtype)
    lane = lax.broadcasted_iota(jnp.int32, roped.shape, 1)
    ones_col = jnp.where(lane == 0, 1.0, 0.0).astype(v_ref.dtype)
    for h in range(hb):
        c0 = h * QK_PAD
        k_ref[:, c0:c0 + LANES] = kv[:, c0:c0 + LANES].astype(k_ref.dtype)
        k_ref[:, c0 + LANES:c0 + QK_PAD] = roped
        v_ref[:, c0:c0 + V_HEAD] = kv[:, c0 + LANES:c0 + QK_PAD].astype(v_ref.dtype)
        v_ref[:, c0 + V_HEAD:c0 + V_PAD] = ones_col


def _rope_tables(seq):
    half = QK_ROPE // 2
    inv_freq = ROPE_THETA ** (-jnp.arange(0, half, dtype=F32) / half)
    ang = jnp.arange(seq, dtype=jnp.int32).astype(F32)[:, None] * inv_freq[None, :]
    cos, sin = jnp.cos(ang), jnp.sin(ang)
    zeros = jnp.zeros((seq, LANES - QK_ROPE), F32)
    return jnp.concatenate([cos, cos, zeros], axis=1), jnp.concatenate([-sin, sin, zeros], axis=1)


def _q_proj(proj, cq_off, q_norm_w, w_uq, cs1, cs2, seq, heads, bm, hb):
    m = proj.shape[0]
    ql = w_uq.shape[0]
    w = w_uq.reshape(ql, heads, QK_NOPE + QK_ROPE)
    nope, rp = w[:, :, :QK_NOPE], w[:, :, QK_NOPE:]
    half = QK_ROPE // 2
    wq = jnp.concatenate([nope, rp, rp[:, :, half:], rp[:, :, :half]], axis=2).reshape(ql, heads * QK_PAD).astype(BF16)
    nb = seq // bm
    scale = LOG2E / math.sqrt(QK_NOPE + QK_ROPE)
    cq_specs = _piece_specs(cq_off, ql, bm)
    return pl.pallas_call(
        functools.partial(_q_kernel, hb=hb, scale=scale),
        grid=(heads // hb, m // bm),
        in_specs=cq_specs + [
            pl.BlockSpec((1, ql), lambda j, i: (0, 0)),
            pl.BlockSpec((ql, hb * QK_PAD), lambda j, i: (0, j)),
            pl.BlockSpec((bm, LANES), lambda j, i: (i % nb, 0)),
            pl.BlockSpec((bm, LANES), lambda j, i: (i % nb, 0)),
        ],
        out_specs=pl.BlockSpec((bm, hb * QK_PAD), lambda j, i: (i, j)),
        out_shape=jax.ShapeDtypeStruct((m, heads * QK_PAD), BF16),
        compiler_params=_params("parallel", "parallel"),
        name="q_proj",
    )(*([proj] * len(cq_specs)), q_norm_w.astype(F32).reshape(1, ql), wq, cs1, cs2)


def _kv_proj(proj, kvc_off, kr, kv_norm_w, w_ukv, cs1, cs2, seq, heads, bm, hb):
    m = proj.shape[0]
    kvl = w_ukv.shape[0]
    nb = seq // bm
    kvc_specs = _piece_specs(kvc_off, kvl, bm)
    return pl.pallas_call(
        functools.partial(_kv_kernel, hb=hb),
        grid=(heads // hb, m // bm),
        in_specs=kvc_specs + [
            pl.BlockSpec((bm, LANES), lambda j, i: (i, 0)),
            pl.BlockSpec((1, kvl), lambda j, i: (0, 0)),
            pl.BlockSpec((kvl, hb * QK_PAD), lambda j, i: (0, j)),
            pl.BlockSpec((bm, LANES), lambda j, i: (i % nb, 0)),
            pl.BlockSpec((bm, LANES), lambda j, i: (i % nb, 0)),
        ],
        out_specs=[
            pl.BlockSpec((bm, hb * QK_PAD), lambda j, i: (i, j)),
            pl.BlockSpec((bm, hb * V_PAD), lambda j, i: (i, j)),
        ],
        out_shape=[
            jax.ShapeDtypeStruct((m, heads * QK_PAD), BF16),
            jax.ShapeDtypeStruct((m, heads * V_PAD), BF16),
        ],
        compiler_params=_params("parallel", "parallel"),
        name="kv_proj",
    )(*([proj] * len(kvc_specs)), kr, kv_norm_w.astype(F32).reshape(1, kvl),
      w_ukv.astype(BF16), cs1, cs2)


def _attn_kernel(q_ref, k_ref, v_ref, g_ref, o_ref, m_ref, acc_ref, s_ref, *, bq, bk, hpb):
    qi = pl.program_id(2)
    m_ref[...] = jnp.full(m_ref.shape, MASK_VALUE, F32)
    acc_ref[...] = jnp.zeros_like(acc_ref)

    def scores(j, slot, r0):
        start = pl.multiple_of(j * bk, bk)
        for h in range(hpb):
            s_ref[slot, h, r0:bq, :] = lax.dot_general(
                q_ref[r0:bq, h * QK_PAD:(h + 1) * QK_PAD], k_ref[pl.ds(start, bk), h * QK_PAD:(h + 1) * QK_PAD],
                _NT, preferred_element_type=F32)

    def accumulate(j, slot, r0, masked):
        start = pl.multiple_of(j * bk, bk)
        rows = bq - r0
        for h in range(hpb):
            s = s_ref[slot, h, r0:bq, :]
            v = v_ref[pl.ds(start, bk), h * V_PAD:(h + 1) * V_PAD]
            if masked:
                ri = lax.broadcasted_iota(jnp.int32, (rows, bk), 0)
                ci = lax.broadcasted_iota(jnp.int32, (rows, bk), 1)
                s = jnp.where(ci <= ri, s, MASK_VALUE)
            m_prev = m_ref[h, r0:bq, :]
            m_next = jnp.maximum(m_prev, jnp.max(s, axis=1, keepdims=True))
            p = jnp.exp2(s - jnp.concatenate([m_next] * (bk // LANES), axis=1))
            alpha = jnp.exp2(m_prev - m_next)
            pv = jnp.dot(p.astype(v.dtype), v, preferred_element_type=F32)
            acc_ref[h, r0:bq, :] = jnp.concatenate([alpha] * (V_PAD // LANES), axis=1) * acc_ref[h, r0:bq, :] + pv
            m_ref[h, r0:bq, :] = m_next

    scores(0, 0, 0)

    def body(i, carry):
        j = 2 * i
        scores(j + 1, 1, 0)
        accumulate(j, 0, 0, False)
        scores(j + 2, 0, 0)
        accumulate(j + 1, 1, 0, False)
        return carry

    nsub = bq // bk
    lax.fori_loop(0, qi * (nsub // 2), body, 0)
    for t in range(nsub):
        if t + 1 < nsub:
            scores(qi * nsub + t + 1, (t + 1) % 2, (t + 1) * bk)
        accumulate(qi * nsub + t, t % 2, t * bk, True)

    for h in range(hpb):
        acc = acc_ref[h]
        g = g_ref[:, h * V_HEAD:(h + 1) * V_HEAD].astype(F32)
        o = acc[:, 0:V_HEAD] / acc[:, V_HEAD:V_HEAD + 1]
        o_ref[:, h * V_HEAD:(h + 1) * V_HEAD] = (o * _silu(g)).astype(o_ref.dtype)


def _attention(q, k, v, proj, g_off, batch, seq, heads, bq, bk, hpb):
    m = q.shape[0]
    nq = seq // bq
    assert seq % bq == 0 and bq % (2 * bk) == 0 and g_off % (hpb * V_HEAD) == 0
    g0 = g_off // (hpb * V_HEAD)
    return pl.pallas_call(
        functools.partial(_attn_kernel, bq=bq, bk=bk, hpb=hpb),
        grid=(batch, heads // hpb, nq),
        in_specs=[
            pl.BlockSpec((bq, hpb * QK_PAD), lambda b, h, i: (b * nq + i, h)),
            pl.BlockSpec((seq, hpb * QK_PAD), lambda b, h, i: (b, h)),
            pl.BlockSpec((seq, hpb * V_PAD), lambda b, h, i: (b, h)),
            pl.BlockSpec((bq, hpb * V_HEAD), lambda b, h, i: (b * nq + i, g0 + h)),
        ],
        out_specs=pl.BlockSpec((bq, hpb * V_HEAD), lambda b, h, i: (b * nq + i, h)),
        out_shape=jax.ShapeDtypeStruct((m, heads * V_HEAD), BF16),
        scratch_shapes=[
            pltpu.VMEM((hpb, bq, LANES), F32),
            pltpu.VMEM((hpb, bq, V_PAD), F32),
            pltpu.VMEM((2, hpb, bq, bk), F32),
        ],
        compiler_params=_params("parallel", "parallel", "arbitrary"),
        name="mla_attention",
    )(q, k, v, proj)


def _merge_kernel(ys_ref, ya_ref, ws_ref, wa_ref, gs_ref, ga_ref, o_ref):
    a = jnp.dot(ys_ref[...], ws_ref[...], preferred_element_type=F32)
    b = jnp.dot(ya_ref[...], wa_ref[...], preferred_element_type=F32)
    o = _sigmoid(gs_ref[...].astype(F32)) * a + _sigmoid(ga_ref[...].astype(F32)) * b
    o_ref[...] = o.astype(o_ref.dtype)


def _merge(y_ssm, y_attn, w_s, w_a, proj, off, bm, bn):
    m, ds = y_ssm.shape
    da = y_attn.shape[1]
    d = w_s.shape[1]
    gs0, ga0 = off["gate_ssm"] // bn, off["gate_attn"] // bn
    return pl.pallas_call(
        _merge_kernel,
        grid=(d // bn, m // bm),
        in_specs=[
            pl.BlockSpec((bm, ds), lambda j, i: (i, 0)),
            pl.BlockSpec((bm, da), lambda j, i: (i, 0)),
            pl.BlockSpec((ds, bn), lambda j, i: (0, j)),
            pl.BlockSpec((da, bn), lambda j, i: (0, j)),
            pl.BlockSpec((bm, bn), lambda j, i: (i, gs0 + j)),
            pl.BlockSpec((bm, bn), lambda j, i: (i, ga0 + j)),
        ],
        out_specs=pl.BlockSpec((bm, bn), lambda j, i: (i, j)),
        out_shape=jax.ShapeDtypeStruct((m, d), BF16),
        compiler_params=_params("parallel", "parallel"),
        name="merge_proj",
    )(y_ssm, y_attn, w_s, w_a, proj, proj)


def _out_kernel(a_ref, b_ref, x_ref, nw_ref, o_ref, *, nj, bn, final_norm):
    j = pl.program_id(1)
    hblk = x_ref[...] + jnp.dot(a_ref[...], b_ref[...], preferred_element_type=F32)
    for jj in range(nj):
        @pl.when(j == jj)
        def _(jj=jj):
            o_ref[:, jj * bn:(jj + 1) * bn] = hblk

    if final_norm:
        @pl.when(j == nj - 1)
        def _():
            hf = o_ref[...]
            ms = jnp.mean(hf * hf, axis=-1, keepdims=True)
            o_ref[...] = (hf * lax.rsqrt(ms + EPS)) * nw_ref[...]


def _out_proj(merged, w_out, x, norm_w, final_norm, bm, bn):
    m, k = merged.shape
    n = w_out.shape[1]
    nj = n // bn
    return pl.pallas_call(
        functools.partial(_out_kernel, nj=nj, bn=bn, final_norm=final_norm),
        grid=(m // bm, nj),
        in_specs=[
            pl.BlockSpec((bm, k), lambda i, j: (i, 0)),
            pl.BlockSpec((k, bn), lambda i, j: (0, j)),
            pl.BlockSpec((bm, bn), lambda i, j: (i, j)),
            pl.BlockSpec((1, n), lambda i, j: (0, 0)),
        ],
        out_specs=pl.BlockSpec((bm, n), lambda i, j: (i, 0)),
        out_shape=jax.ShapeDtypeStruct((m, n), F32),
        compiler_params=_params("parallel", "arbitrary"),
        name="out_proj",
    )(merged, w_out, x, norm_w.astype(F32).reshape(1, n))


def _in_proj_layout(n_cols, d_ssm, groups, heads_ssm, q_lora, kv_lora, d_attn, d_model, bn_in):
    gn = groups * SSM_STATE
    sizes = (d_ssm, d_ssm + 2 * gn, heads_ssm, q_lora, kv_lora + QK_ROPE, d_attn, d_model, d_model)
    starts = [0]
    for s in sizes[:-1]:
        starts.append(starts[-1] + s)
    z0, xbc0, dt0, cq0, kv0, ga0, gs0, gt0 = starts
    n_head = -(-(kv0 + kv_lora) // bn_in)
    n_gate = (d_attn + 2 * d_model) // bn_in
    assert (d_attn + 2 * d_model) % bn_in == 0 and n_head * bn_in <= n_cols and gt0 + d_model == n_cols
    tile_cols = [t * bn_in for t in range(n_head)] + [ga0 + t * bn_in for t in range(n_gate)]
    gates0 = n_head * bn_in
    off = {"z": z0, "xs": xbc0, "B": xbc0 + d_ssm, "C": xbc0 + d_ssm + gn, "cq": cq0, "kvc": kv0,
           "g_attn": gates0, "gate_ssm": gates0 + d_attn, "gate_attn": gates0 + d_attn + d_model}
    return tile_cols, off, dt0, kv0 + kv_lora


def kernel(x, norm_in_w, w_in, conv_w, conv_b, dt_bias, a_log, d_skip, ssm_norm_w, q_norm_w, w_uq,
           kv_norm_w, w_ukv, w_branch_ssm, w_branch_attn, w_out, norm_final_w):
    batch, seq, d_model = x.shape
    depth = w_in.shape[0]
    m = batch * seq
    d_ssm = w_branch_ssm.shape[1]
    d_attn = w_branch_attn.shape[1]
    heads_ssm = dt_bias.shape[1]
    groups = (conv_w.shape[2] - d_ssm) // (2 * SSM_STATE)
    gw = (heads_ssm // groups) * SSM_HEADDIM
    q_lora, kv_lora = w_uq.shape[1], w_ukv.shape[1]
    heads = d_attn // V_HEAD
    assert heads_ssm * SSM_HEADDIM == d_ssm and w_uq.shape[2] == heads * (QK_NOPE + QK_ROPE)

    bm_norm = min(256, m)
    bm_in, bn_in = min(1024, m), 1024
    cps_ssd = 4
    bm_qkv, hb_qkv = min(512, seq), 8
    bq_attn, bk_attn, hpb_attn = min(2048, seq), min(512, seq // 2), 1
    bm_merge, bn_merge = min(512, m), 512
    bm_out, bn_out = min(512, m), 512

    cs1, cs2 = _rope_tables(seq)
    wt_in = jnp.swapaxes(w_in, 1, 2)
    tile_cols, off, dt_col, kr_col = _in_proj_layout(w_in.shape[2], d_ssm, groups, heads_ssm, q_lora, kv_lora,
                                                     d_attn, d_model, bn_in)
    piece = math.gcd(bn_in, *tile_cols[1:])
    off = {name: col + bn_in for name, col in off.items()}
    h = x.reshape(m, d_model)
    for layer in range(depth):
        u = _rmsnorm(h, norm_in_w[layer], BF16, bm_norm, "norm_in")
        proj = _matmul_wt(u, wt_in, layer, tile_cols, piece, BF16, bm_in, bn_in, "in_proj")
        kr = _kr_proj(u, wt_in, layer, kr_col, bm_in)

        dt, acs, acst = _dt_prep(u, wt_in, layer, dt_col, dt_bias[layer].astype(F32), a_log[layer].astype(F32))
        y_ssm = _ssd(proj, off, dt, acs, acst, conv_w[layer], conv_b[layer], d_skip[layer],
                     ssm_norm_w[layer], batch, seq, groups, d_ssm, cps_ssd)

        q = _q_proj(proj, off["cq"], q_norm_w[layer], w_uq[layer], cs1, cs2, seq, heads, bm_qkv, hb_qkv)
        k, v = _kv_proj(proj, off["kvc"], kr, kv_norm_w[layer], w_ukv[layer], cs1, cs2, seq, heads, bm_qkv, hb_qkv)
        y_attn = _attention(q, k, v, proj, off["g_attn"], batch, seq, heads, bq_attn, bk_attn, hpb_attn)

        merged = _merge(y_ssm, y_attn, w_branch_ssm[layer].astype(BF16), w_branch_attn[layer].astype(BF16),
                        proj, off, bm_merge, bn_merge)
        last = layer == depth - 1
        h = _out_proj(merged, w_out[layer].astype(BF16), h, norm_final_w, last, bm_out, bn_out)
    return h.reshape(batch, seq, d_model)
```

```python
import functools
import math

import jax
import jax.numpy as jnp
from jax import lax
from jax.experimental import pallas as pl
from jax.experimental.pallas import tpu as pltpu

F32 = jnp.float32
BF16 = jnp.bfloat16

EPS = 1e-6
SSM_HEADDIM = 64
SSM_STATE = 128
CONV_WIDTH = 4
CHUNK = 128
QK_NOPE = 128
QK_ROPE = 64
V_HEAD = 128
ROPE_THETA = 10000.0

LANES = 128
SUBLANES = 8
PACKED_ROWS = 16
VMEM_LIMIT_BYTES = 60 * 1024 * 1024
QK_PAD = 2 * LANES
V_PAD = 2 * LANES
MASK_VALUE = -1e30
LOG2E = math.log2(math.e)


def _params(*sem):
    return pltpu.CompilerParams(dimension_semantics=sem, vmem_limit_bytes=VMEM_LIMIT_BYTES)


def _sigmoid(x):
    return 0.5 + 0.5 * jnp.tanh(0.5 * x)


def _silu(x):
    h = 0.5 * x
    return h + h * jnp.tanh(h)


def _rmsnorm_kernel(x_ref, w_ref, o_ref):
    x = x_ref[...].astype(F32)
    ms = jnp.mean(x * x, axis=-1, keepdims=True)
    o_ref[...] = ((x * lax.rsqrt(ms + EPS)) * w_ref[...]).astype(o_ref.dtype)


def _rmsnorm(x, w, out_dtype, bm, name):
    m, d = x.shape
    return pl.pallas_call(
        _rmsnorm_kernel,
        grid=(m // bm,),
        in_specs=[pl.BlockSpec((bm, d), lambda i: (i, 0)), pl.BlockSpec((1, d), lambda i: (0, 0))],
        out_specs=pl.BlockSpec((bm, d), lambda i: (i, 0)),
        out_shape=jax.ShapeDtypeStruct((m, d), out_dtype),
        compiler_params=_params("parallel"),
        name=name,
    )(x, w.reshape(1, d))


def _matmul_kernel(a_ref, b_ref, o_ref):
    o_ref[...] = jnp.dot(a_ref[...], b_ref[...], preferred_element_type=F32).astype(o_ref.dtype)


def _matmul(a, b, out_dtype, bm, bn, name):
    m, k = a.shape
    n = b.shape[1]
    return pl.pallas_call(
        _matmul_kernel,
        grid=(n // bn, m // bm),
        in_specs=[pl.BlockSpec((bm, k), lambda j, i: (i, 0)), pl.BlockSpec((k, bn), lambda j, i: (0, j))],
        out_specs=pl.BlockSpec((bm, bn), lambda j, i: (i, j)),
        out_shape=jax.ShapeDtypeStruct((m, n), out_dtype),
        compiler_params=_params("parallel", "parallel"),
        name=name,
    )(a, b)


_NT = (((1,), (1,)), ((), ()))


def _matmul_wt_kernel(*refs, pps, piece, n_tiles):
    a_ref, w_refs, o_ref, w16_ref = refs[1], refs[2:2 + pps], refs[2 + pps], refs[3 + pps]
    j = pl.program_id(0)
    i = pl.program_id(1)
    npiece = w16_ref.shape[1] // piece

    @pl.when(j < n_tiles)
    def _():
        for p, w_ref in enumerate(w_refs):
            t = i * pps + p

            @pl.when(t < npiece)
            def _(w_ref=w_ref, t=t):
                w16_ref[j % 2, pl.ds(pl.multiple_of(t * piece, piece), piece), :] = w_ref[...].astype(BF16)

    @pl.when(j == 0)
    def _():
        o_ref[...] = jnp.zeros_like(o_ref)

    @pl.when(j > 0)
    def _():
        o_ref[...] = lax.dot_general(a_ref[...], w16_ref[(j - 1) % 2], _NT,
                                     preferred_element_type=F32).astype(o_ref.dtype)


def _matmul_wt(a, wt, layer, tile_rows, piece, out_dtype, bm, bn, name):
    m, k = a.shape
    npiece = bn // piece
    n_tiles = len(tile_rows)
    n_row = m // bm
    pps = -(-npiece // n_row)
    starts = jnp.asarray(tile_rows, jnp.int32) // piece
    assert all(r % piece == 0 for r in tile_rows)

    def piece_spec(p):
        def index(j, i, s):
            return (layer, s[jnp.minimum(j, n_tiles - 1)] + jnp.minimum(i * pps + p, npiece - 1), 0)
        return pl.BlockSpec((None, piece, k), index)

    return pl.pallas_call(
        functools.partial(_matmul_wt_kernel, pps=pps, piece=piece, n_tiles=n_tiles),
        grid_spec=pltpu.PrefetchScalarGridSpec(
            num_scalar_prefetch=1,
            grid=(n_tiles + 1, n_row),
            in_specs=[pl.BlockSpec((bm, k), lambda j, i, s: (i, 0))] + [piece_spec(p) for p in range(pps)],
            out_specs=pl.BlockSpec((bm, bn), lambda j, i, s: (i, j)),
            scratch_shapes=[pltpu.VMEM((2, bn, k), BF16)],
        ),
        out_shape=jax.ShapeDtypeStruct((m, (n_tiles + 1) * bn), out_dtype),
        compiler_params=_params("arbitrary", "arbitrary"),
        name=name,
    )(starts, a, *([wt] * pps))


def _dt_kernel(u_ref, wdt_ref, wkr_ref, bias_ref, alog_ref, dt_ref, acs_ref, acst_ref, kr_ref, *, cps):
    u = u_ref[...]
    raw = lax.dot_general(u, wdt_ref[...].astype(BF16), _NT, preferred_element_type=F32) + bias_ref[...]
    dt = jnp.maximum(raw, 0.0) + jnp.log1p(jnp.exp(-jnp.abs(raw)))
    a = dt * (-jnp.exp(alog_ref[...]))
    dt_ref[...] = dt
    li = lax.broadcasted_iota(jnp.int32, (CHUNK, CHUNK), 0)
    ki = lax.broadcasted_iota(jnp.int32, (CHUNK, CHUNK), 1)
    tril = (ki <= li).astype(F32)
    for c in range(cps):
        acs = jnp.dot(tril, a[c * CHUNK:(c + 1) * CHUNK], precision=lax.Precision.HIGHEST,
                      preferred_element_type=F32)
        acs_ref[c * CHUNK:(c + 1) * CHUNK, :] = acs
        acst_ref[c] = acs.T
    w = wkr_ref[...]
    half = QK_ROPE // 2
    w_all = jnp.concatenate([w, w[half:], w[:half]], axis=0).astype(BF16)
    kr_ref[...] = lax.dot_general(u, w_all, _NT, preferred_element_type=F32).astype(kr_ref.dtype)


def _dt_prep(u, wt, layer, dt_row0, kr_row0, dt_bias, a_log, cps):
    m, d = u.shape
    h = dt_bias.shape[0]
    nc = m // CHUNK
    rows = cps * CHUNK
    assert dt_row0 % h == 0 and kr_row0 % QK_ROPE == 0 and m % rows == 0
    return pl.pallas_call(
        functools.partial(_dt_kernel, cps=cps),
        grid=(m // rows,),
        in_specs=[
            pl.BlockSpec((rows, d), lambda i: (i, 0)),
            pl.BlockSpec((None, h, d), lambda i: (layer, dt_row0 // h, 0)),
            pl.BlockSpec((None, QK_ROPE, d), lambda i: (layer, kr_row0 // QK_ROPE, 0)),
            pl.BlockSpec((1, h), lambda i: (0, 0)),
            pl.BlockSpec((1, h), lambda i: (0, 0)),
        ],
        out_specs=[
            pl.BlockSpec((rows, h), lambda i: (i, 0)),
            pl.BlockSpec((rows, h), lambda i: (i, 0)),
            pl.BlockSpec((cps, h, CHUNK), lambda i: (i, 0, 0)),
            pl.BlockSpec((rows, LANES), lambda i: (i, 0)),
        ],
        out_shape=[
            jax.ShapeDtypeStruct((m, h), F32),
            jax.ShapeDtypeStruct((m, h), F32),
            jax.ShapeDtypeStruct((nc, h, CHUNK), F32),
            jax.ShapeDtypeStruct((m, LANES), BF16),
        ],
        compiler_params=_params("parallel"),
        name="dt_prep",
    )(u, wt, wt, dt_bias.reshape(1, h), a_log.reshape(1, h))


def _rope_tile(t, cs1, cs2):
    return t * cs1 + pltpu.roll(t, LANES // 2, 1) * cs2


def _split_expand(v, onehot2_ref):
    hi = v.astype(BF16)
    mid = (v - hi.astype(F32)).astype(BF16)
    return jnp.dot(jnp.concatenate([hi, mid], axis=1), onehot2_ref[...], preferred_element_type=F32)


def _ssd_kernel(xs_ref, b_ref, c_ref, z_ref, dt_ref, acs_ref, acst_ref, cw_ref, cbias_ref, shift_ref,
                rexp_ref, rexp2_ref, eexp2_ref, dskip_ref, nw_ref,
                y_ref, pad_ref, state_ref, ydiag_ref, *, r, cps):
    gw = r * SSM_HEADDIM
    n = SSM_STATE
    halo = PACKED_ROWS
    rows_step = cps * CHUNK

    @pl.when(pl.program_id(2) == 0)
    def _():
        pad_ref[0:halo, :] = jnp.zeros((halo, gw + 2 * n), BF16)
        state_ref[...] = jnp.zeros_like(state_ref)

    pad_ref[halo:halo + rows_step, 0:gw] = xs_ref[...]
    pad_ref[halo:halo + rows_step, gw:gw + n] = b_ref[...]
    pad_ref[halo:halo + rows_step, gw + n:gw + 2 * n] = c_ref[...]
    li = lax.broadcasted_iota(jnp.int32, (CHUNK, CHUNK), 0)
    si = lax.broadcasted_iota(jnp.int32, (CHUNK, CHUNK), 1)
    causal = si <= li
    low_half = si < SSM_HEADDIM

    for sub in range(cps):
        r0 = sub * CHUNK
        window = pad_ref[r0:r0 + halo + CHUNK, :]
        shifted = jnp.dot(shift_ref[...], window, preferred_element_type=F32)
        conv = cbias_ref[0]
        for k in range(CONV_WIDTH - 1):
            conv = conv + shifted[k * CHUNK:(k + 1) * CHUNK] * cw_ref[0, k:k + 1, :]
        conv = conv + window[halo:halo + CHUNK, :].astype(F32) * cw_ref[0, CONV_WIDTH - 1:CONV_WIDTH, :]
        xc = _silu(conv[:, 0:gw])
        bc = _silu(conv[:, gw:gw + n])
        cc = _silu(conv[:, gw + n:gw + 2 * n])

        cb16 = cc.astype(BF16)
        bb16 = bc.astype(BF16)
        cbm = lax.dot_general(cb16, bb16, _NT, preferred_element_type=F32)

        acs = acs_ref[r0:r0 + CHUNK, :]
        acs_last = acs[CHUNK - 1:CHUNK, :]
        per_head = jnp.concatenate([dt_ref[r0:r0 + CHUNK, :], jnp.exp(acs), jnp.exp(acs_last - acs)], axis=0)
        ex = jnp.dot(per_head.astype(BF16), rexp_ref[...], preferred_element_type=F32)
        dt_e = ex[0:CHUNK]
        eacs_e = ex[CHUNK:2 * CHUNK]
        dst_e = ex[2 * CHUNK:3 * CHUNK]
        last_e = _split_expand(jnp.broadcast_to(acs_last, (PACKED_ROWS, acs.shape[1])), rexp2_ref)
        cd_e = jnp.exp(last_e[0:1, :])
        col_all = _split_expand(acs, eexp2_ref)

        x32 = xc * dt_e
        x16 = x32.astype(BF16)
        zero16 = jnp.zeros((CHUNK, LANES), BF16)
        for p in range(r // 2):
            xp = x16[:, p * LANES:(p + 1) * LANES]
            decays = []
            for q in range(2):
                j = 2 * p + q
                seg = col_all[:, j * LANES:(j + 1) * LANES] - acst_ref[sub, j:j + 1, :]
                decays.append((jnp.exp(jnp.where(causal, seg, -jnp.inf)) * cbm).astype(BF16))
            x_pair = jnp.concatenate([jnp.where(low_half, xp, zero16), jnp.where(low_half, zero16, xp)], axis=0)
            ydiag_ref[sub, :, p * LANES:(p + 1) * LANES] = jnp.dot(
                jnp.concatenate(decays, axis=1), x_pair, preferred_element_type=F32)

        st = state_ref[...]
        y_off = jnp.dot(cb16, st.astype(BF16), preferred_element_type=F32) * eacs_e
        xd16 = (x32 * dst_e).astype(BF16)
        state_ref[...] = st * cd_e + jnp.dot(bc.T.astype(BF16), xd16, preferred_element_type=F32)

        y = ydiag_ref[sub] + y_off + xc * dskip_ref[...]
        gv = y * _silu(z_ref[r0:r0 + CHUNK, :].astype(F32))
        ms = jnp.mean(gv * gv, axis=-1, keepdims=True)
        y_ref[r0:r0 + CHUNK, :] = ((gv * lax.rsqrt(ms + EPS)) * nw_ref[...]).astype(y_ref.dtype)

    pad_ref[0:halo, :] = pad_ref[rows_step:rows_step + halo, :]


def _ssd(proj, off, dt, acs, acst, conv_w, conv_b, d_skip, ssm_norm_w, batch, seq, groups, d_ssm, cps):
    m = proj.shape[0]
    heads = dt.shape[1]
    r = heads // groups
    gw = r * SSM_HEADDIM
    n = SSM_STATE
    nc = seq // CHUNK
    assert heads == LANES and r % 2 == 0 and gw % LANES == 0 and seq % (cps * CHUNK) == 0

    head_of_col = jnp.arange(d_ssm, dtype=jnp.int32) // SSM_HEADDIM
    rexp = (jnp.arange(heads, dtype=jnp.int32)[:, None] == head_of_col[None, :]).astype(BF16)
    head_of_tile = jnp.arange(heads * LANES, dtype=jnp.int32) // LANES
    eexp = (jnp.arange(heads, dtype=jnp.int32)[:, None] == head_of_tile[None, :]).astype(BF16)
    rexp2 = jnp.concatenate([rexp] * 2, axis=0)
    eexp2 = jnp.concatenate([eexp] * 2, axis=0)
    dskip_e = jnp.repeat(d_skip.astype(F32), SSM_HEADDIM).reshape(1, d_ssm)

    def per_group(a):
        rows = a.shape[0]
        xs = a[:, :d_ssm].reshape(rows, groups, gw)
        bs = a[:, d_ssm:d_ssm + groups * n].reshape(rows, groups, n)
        cs = a[:, d_ssm + groups * n:].reshape(rows, groups, n)
        return jnp.concatenate([xs, bs, cs], axis=2).transpose(1, 0, 2)

    cw = per_group(conv_w.astype(F32))
    cb = per_group(conv_b.astype(F32).reshape(1, -1))
    halo = PACKED_ROWS
    t_idx = jnp.arange((CONV_WIDTH - 1) * CHUNK, dtype=jnp.int32)
    src = halo + (t_idx % CHUNK) - (CONV_WIDTH - 1) + t_idx // CHUNK
    shift = (src[:, None] == jnp.arange(halo + CHUNK, dtype=jnp.int32)[None, :]).astype(BF16)

    ns = nc // cps
    rows = cps * CHUNK

    def row(b, g, c):
        return b * ns + c

    xs0, b0, c0, z0 = off["xs"] // gw, off["B"] // n, off["C"] // n, off["z"] // gw
    cols = gw + 2 * n
    in_specs = [
        pl.BlockSpec((rows, gw), lambda b, g, c: (row(b, g, c), xs0 + g)),
        pl.BlockSpec((rows, n), lambda b, g, c: (row(b, g, c), b0 + g)),
        pl.BlockSpec((rows, n), lambda b, g, c: (row(b, g, c), c0 + g)),
        pl.BlockSpec((rows, gw), lambda b, g, c: (row(b, g, c), z0 + g)),
        pl.BlockSpec((rows, heads), lambda b, g, c: (row(b, g, c), 0)),
        pl.BlockSpec((rows, heads), lambda b, g, c: (row(b, g, c), 0)),
        pl.BlockSpec((cps, r, CHUNK), lambda b, g, c: (row(b, g, c), g, 0)),
        pl.BlockSpec((1, CONV_WIDTH, cols), lambda b, g, c: (g, 0, 0)),
        pl.BlockSpec((1, 1, cols), lambda b, g, c: (g, 0, 0)),
        pl.BlockSpec(((CONV_WIDTH - 1) * CHUNK, halo + CHUNK), lambda b, g, c: (0, 0)),
        pl.BlockSpec((heads, gw), lambda b, g, c: (0, g)),
        pl.BlockSpec((2 * heads, gw), lambda b, g, c: (0, g)),
        pl.BlockSpec((2 * heads, r * LANES), lambda b, g, c: (0, g)),
        pl.BlockSpec((1, gw), lambda b, g, c: (0, g)),
        pl.BlockSpec((1, gw), lambda b, g, c: (0, g)),
    ]
    return pl.pallas_call(
        functools.partial(_ssd_kernel, r=r, cps=cps),
        grid=(batch, groups, ns),
        in_specs=in_specs,
        out_specs=pl.BlockSpec((rows, gw), lambda b, g, c: (row(b, g, c), g)),
        out_shape=jax.ShapeDtypeStruct((m, d_ssm), BF16),
        scratch_shapes=[
            pltpu.VMEM((halo + rows, cols), BF16),
            pltpu.VMEM((n, gw), F32),
            pltpu.VMEM((cps, CHUNK, gw), F32),
        ],
        compiler_params=_params("parallel", "parallel", "arbitrary"),
        name="ssd_scan",
    )(proj, proj, proj, proj, dt, acs, acst, cw, cb, shift,
      rexp, rexp2, eexp2, dskip_e, ssm_norm_w.astype(F32).reshape(1, d_ssm))


def _normed(x_refs, w_ref):
    x = jnp.concatenate([r[...] for r in x_refs], axis=1).astype(F32)
    ms = jnp.mean(x * x, axis=-1, keepdims=True)
    return ((x * lax.rsqrt(ms + EPS)) * w_ref[...]).astype(BF16)


def _piece_specs(offset, width, bm):
    piece = math.gcd(offset, width)
    assert piece % LANES == 0
    first = offset // piece
    return [pl.BlockSpec((bm, piece), lambda j, i, t=t: (i, first + t)) for t in range(width // piece)]


def _q_kernel(*refs, hb, scale):
    nw_ref, wq_ref, cs1_ref, cs2_ref, o_ref = refs[-5:]
    cqn = _normed(refs[:-5], nw_ref)
    q = jnp.dot(cqn, wq_ref[...], preferred_element_type=F32)
    cs1 = cs1_ref[...] * scale
    cs2 = cs2_ref[...] * scale
    for h in range(hb):
        c0 = h * QK_PAD
        o_ref[:, c0:c0 + LANES] = (q[:, c0:c0 + LANES] * scale).astype(o_ref.dtype)
        o_ref[:, c0 + LANES:c0 + QK_PAD] = _rope_tile(q[:, c0 + LANES:c0 + QK_PAD], cs1, cs2).astype(o_ref.dtype)


def _kv_kernel(*refs, hb):
    kr_ref, nw_ref, w_ref, cs1_ref, cs2_ref, k_ref, v_ref = refs[-7:]
    ckv = _normed(refs[:-7], nw_ref)
    kv = jnp.dot(ckv, w_ref[...], preferred_element_type=F32)
    roped = _rope_tile(kr_ref[...].astype(F32), cs1_ref[...], cs2_ref[...]).astype(k_ref.dtype)
    lane = lax.broadcasted_iota(jnp.int32, roped.shape, 1)
    ones_col = jnp.where(lane == 0, 1.0, 0.0).astype(v_ref.dtype)
    for h in range(hb):
        c0 = h * QK_PAD
        k_ref[:, c0:c0 + LANES] = kv[:, c0:c0 + LANES].astype(k_ref.dtype)
        k_ref[:, c0 + LANES:c0 + QK_PAD] = roped
        v_ref[:, c0:c0 + V_HEAD] = kv[:, c0 + LANES:c0 + QK_PAD].astype(v_ref.dtype)
        v_ref[:, c0 + V_HEAD:c0 + V_PAD] = ones_col


def _rope_tables(seq):
    half = QK_ROPE // 2
    inv_freq = ROPE_THETA ** (-jnp.arange(0, half, dtype=F32) / half)
    ang = jnp.arange(seq, dtype=jnp.int32).astype(F32)[:, None] * inv_freq[None, :]
    cos, sin = jnp.cos(ang), jnp.sin(ang)
    zeros = jnp.zeros((seq, LANES - QK_ROPE), F32)
    return jnp.concatenate([cos, cos, zeros], axis=1), jnp.concatenate([-sin, sin, zeros], axis=1)


def _q_proj(proj, cq_off, q_norm_w, w_uq, cs1, cs2, seq, heads, bm, hb):
    m = proj.shape[0]
    ql = w_uq.shape[0]
    w = w_uq.reshape(ql, heads, QK_NOPE + QK_ROPE)
    nope, rp = w[:, :, :QK_NOPE], w[:, :, QK_NOPE:]
    half = QK_ROPE // 2
    wq = jnp.concatenate([nope, rp, rp[:, :, half:], rp[:, :, :half]], axis=2).reshape(ql, heads * QK_PAD).astype(BF16)
    nb = seq // bm
    scale = LOG2E / math.sqrt(QK_NOPE + QK_ROPE)
    cq_specs = _piece_specs(cq_off, ql, bm)
    return pl.pallas_call(
        functools.partial(_q_kernel, hb=hb, scale=scale),
        grid=(heads // hb, m // bm),
        in_specs=cq_specs + [
            pl.BlockSpec((1, ql), lambda j, i: (0, 0)),
            pl.BlockSpec((ql, hb * QK_PAD), lambda j, i: (0, j)),
            pl.BlockSpec((bm, LANES), lambda j, i: (i % nb, 0)),
            pl.BlockSpec((bm, LANES), lambda j, i: (i % nb, 0)),
        ],
        out_specs=pl.BlockSpec((bm, hb * QK_PAD), lambda j, i: (i, j)),
        out_shape=jax.ShapeDtypeStruct((m, heads * QK_PAD), BF16),
        compiler_params=_params("parallel", "parallel"),
        name="q_proj",
    )(*([proj] * len(cq_specs)), q_norm_w.astype(F32).reshape(1, ql), wq, cs1, cs2)


def _kv_proj(proj, kvc_off, kr, kv_norm_w, w_ukv, cs1, cs2, seq, heads, bm, hb):
    m = proj.shape[0]
    kvl = w_ukv.shape[0]
    nb = seq // bm
    kvc_specs = _piece_specs(kvc_off, kvl, bm)
    return pl.pallas_call(
        functools.partial(_kv_kernel, hb=hb),
        grid=(heads // hb, m // bm),
        in_specs=kvc_specs + [
            pl.BlockSpec((bm, LANES), lambda j, i: (i, 0)),
            pl.BlockSpec((1, kvl), lambda j, i: (0, 0)),
            pl.BlockSpec((kvl, hb * QK_PAD), lambda j, i: (0, j)),
            pl.BlockSpec((bm, LANES), lambda j, i: (i % nb, 0)),
            pl.BlockSpec((bm, LANES), lambda j, i: (i % nb, 0)),
        ],
        out_specs=[
            pl.BlockSpec((bm, hb * QK_PAD), lambda j, i: (i, j)),
            pl.BlockSpec((bm, hb * V_PAD), lambda j, i: (i, j)),
        ],
        out_shape=[
            jax.ShapeDtypeStruct((m, heads * QK_PAD), BF16),
            jax.ShapeDtypeStruct((m, heads * V_PAD), BF16),
        ],
        compiler_params=_params("parallel", "parallel"),
        name="kv_proj",
    )(*([proj] * len(kvc_specs)), kr, kv_norm_w.astype(F32).reshape(1, kvl),
      w_ukv.astype(BF16), cs1, cs2)


def _attn_kernel(q_ref, k_ref, v_ref, g_ref, o_ref, m_ref, acc_ref, s_ref, *, bq, bk, hpb):
    qi = pl.program_id(2)
    m_ref[...] = jnp.full(m_ref.shape, MASK_VALUE, F32)
    acc_ref[...] = jnp.zeros_like(acc_ref)

    def scores(j, slot, r0):
        start = pl.multiple_of(j * bk, bk)
        for h in range(hpb):
            s_ref[slot, h, r0:bq, :] = lax.dot_general(
                q_ref[r0:bq, h * QK_PAD:(h + 1) * QK_PAD], k_ref[pl.ds(start, bk), h * QK_PAD:(h + 1) * QK_PAD],
                _NT, preferred_element_type=F32)

    def accumulate(j, slot, r0, masked):
        start = pl.multiple_of(j * bk, bk)
        rows = bq - r0
        for h in range(hpb):
            s = s_ref[slot, h, r0:bq, :]
            v = v_ref[pl.ds(start, bk), h * V_PAD:(h + 1) * V_PAD]
            if masked:
                ri = lax.broadcasted_iota(jnp.int32, (rows, bk), 0)
                ci = lax.broadcasted_iota(jnp.int32, (rows, bk), 1)
                s = jnp.where(ci <= ri, s, MASK_VALUE)
            m_prev = m_ref[h, r0:bq, :]
            m_next = jnp.maximum(m_prev, jnp.max(s, axis=1, keepdims=True))
            p = jnp.exp2(s - jnp.concatenate([m_next] * (bk // LANES), axis=1))
            alpha = jnp.exp2(m_prev - m_next)
            pv = jnp.dot(p.astype(v.dtype), v, preferred_element_type=F32)
            acc_ref[h, r0:bq, :] = jnp.concatenate([alpha] * (V_PAD // LANES), axis=1) * acc_ref[h, r0:bq, :] + pv
            m_ref[h, r0:bq, :] = m_next

    scores(0, 0, 0)

    def body(i, carry):
        j = 2 * i
        scores(j + 1, 1, 0)
        accumulate(j, 0, 0, False)
        scores(j + 2, 0, 0)
        accumulate(j + 1, 1, 0, False)
        return carry

    nsub = bq // bk
    lax.fori_loop(0, qi * (nsub // 2), body, 0)
    for t in range(nsub):
        if t + 1 < nsub:
            scores(qi * nsub + t + 1, (t + 1) % 2, (t + 1) * bk)
        accumulate(qi * nsub + t, t % 2, t * bk, True)

    for h in range(hpb):
        acc = acc_ref[h]
        g = g_ref[:, h * V_HEAD:(h + 1) * V_HEAD].astype(F32)
        o = acc[:, 0:V_HEAD] / acc[:, V_HEAD:V_HEAD + 1]
        o_ref[:, h * V_HEAD:(h + 1) * V_HEAD] = (o * _silu(g)).astype(o_ref.dtype)


def _attention(q, k, v, proj, g_off, batch, seq, heads, bq, bk, hpb):
    m = q.shape[0]
    nq = seq // bq
    assert seq % bq == 0 and bq % (2 * bk) == 0 and g_off % (hpb * V_HEAD) == 0
    g0 = g_off // (hpb * V_HEAD)
    return pl.pallas_call(
        functools.partial(_attn_kernel, bq=bq, bk=bk, hpb=hpb),
        grid=(batch, heads // hpb, nq),
        in_specs=[
            pl.BlockSpec((bq, hpb * QK_PAD), lambda b, h, i: (b * nq + i, h)),
            pl.BlockSpec((seq, hpb * QK_PAD), lambda b, h, i: (b, h)),
            pl.BlockSpec((seq, hpb * V_PAD), lambda b, h, i: (b, h)),
            pl.BlockSpec((bq, hpb * V_HEAD), lambda b, h, i: (b * nq + i, g0 + h)),
        ],
        out_specs=pl.BlockSpec((bq, hpb * V_HEAD), lambda b, h, i: (b * nq + i, h)),
        out_shape=jax.ShapeDtypeStruct((m, heads * V_HEAD), BF16),
        scratch_shapes=[
            pltpu.VMEM((hpb, bq, LANES), F32),
            pltpu.VMEM((hpb, bq, V_PAD), F32),
            pltpu.VMEM((2, hpb, bq, bk), F32),
        ],
        compiler_params=_params("parallel", "parallel", "arbitrary"),
        name="mla_attention",
    )(q, k, v, proj)


def _merge_kernel(ys_ref, ya_ref, ws_ref, wa_ref, gs_ref, ga_ref, o_ref):
    a = jnp.dot(ys_ref[...], ws_ref[...], preferred_element_type=F32)
    b = jnp.dot(ya_ref[...], wa_ref[...], preferred_element_type=F32)
    o = _sigmoid(gs_ref[...].astype(F32)) * a + _sigmoid(ga_ref[...].astype(F32)) * b
    o_ref[...] = o.astype(o_ref.dtype)


def _merge(y_ssm, y_attn, w_s, w_a, proj, off, bm, bn):
    m, ds = y_ssm.shape
    da = y_attn.shape[1]
    d = w_s.shape[1]
    gs0, ga0 = off["gate_ssm"] // bn, off["gate_attn"] // bn
    return pl.pallas_call(
        _merge_kernel,
        grid=(d // bn, m // bm),
        in_specs=[
            pl.BlockSpec((bm, ds), lambda j, i: (i, 0)),
            pl.BlockSpec((bm, da), lambda j, i: (i, 0)),
            pl.BlockSpec((ds, bn), lambda j, i: (0, j)),
            pl.BlockSpec((da, bn), lambda j, i: (0, j)),
            pl.BlockSpec((bm, bn), lambda j, i: (i, gs0 + j)),
            pl.BlockSpec((bm, bn), lambda j, i: (i, ga0 + j)),
        ],
        out_specs=pl.BlockSpec((bm, bn), lambda j, i: (i, j)),
        out_shape=jax.ShapeDtypeStruct((m, d), BF16),
        compiler_params=_params("parallel", "parallel"),
        name="merge_proj",
    )(y_ssm, y_attn, w_s, w_a, proj, proj)


def _out_kernel(a_ref, b_ref, x_ref, nw_ref, o_ref, *, nj, bn, final_norm):
    j = pl.program_id(1)
    hblk = x_ref[...] + jnp.dot(a_ref[...], b_ref[...], preferred_element_type=F32)
    for jj in range(nj):
        @pl.when(j == jj)
        def _(jj=jj):
            o_ref[:, jj * bn:(jj + 1) * bn] = hblk

    if final_norm:
        @pl.when(j == nj - 1)
        def _():
            hf = o_ref[...]
            ms = jnp.mean(hf * hf, axis=-1, keepdims=True)
            o_ref[...] = (hf * lax.rsqrt(ms + EPS)) * nw_ref[...]


def _out_proj(merged, w_out, x, norm_w, final_norm, bm, bn):
    m, k = merged.shape
    n = w_out.shape[1]
    nj = n // bn
    return pl.pallas_call(
        functools.partial(_out_kernel, nj=nj, bn=bn, final_norm=final_norm),
        grid=(m // bm, nj),
        in_specs=[
            pl.BlockSpec((bm, k), lambda i, j: (i, 0)),
            pl.BlockSpec((k, bn), lambda i, j: (0, j)),
            pl.BlockSpec((bm, bn), lambda i, j: (i, j)),
            pl.BlockSpec((1, n), lambda i, j: (0, 0)),
        ],
        out_specs=pl.BlockSpec((bm, n), lambda i, j: (i, 0)),
        out_shape=jax.ShapeDtypeStruct((m, n), F32),
        compiler_params=_params("parallel", "arbitrary"),
        name="out_proj",
    )(merged, w_out, x, norm_w.astype(F32).reshape(1, n))


def _in_proj_layout(n_cols, d_ssm, groups, heads_ssm, q_lora, kv_lora, d_attn, d_model, bn_in):
    gn = groups * SSM_STATE
    sizes = (d_ssm, d_ssm + 2 * gn, heads_ssm, q_lora, kv_lora + QK_ROPE, d_attn, d_model, d_model)
    starts = [0]
    for s in sizes[:-1]:
        starts.append(starts[-1] + s)
    z0, xbc0, dt0, cq0, kv0, ga0, gs0, gt0 = starts
    n_head = -(-(kv0 + kv_lora) // bn_in)
    n_gate = (d_attn + 2 * d_model) // bn_in
    assert (d_attn + 2 * d_model) % bn_in == 0 and n_head * bn_in <= n_cols and gt0 + d_model == n_cols
    tile_cols = [t * bn_in for t in range(n_head)] + [ga0 + t * bn_in for t in range(n_gate)]
    gates0 = n_head * bn_in
    off = {"z": z0, "xs": xbc0, "B": xbc0 + d_ssm, "C": xbc0 + d_ssm + gn, "cq": cq0, "kvc": kv0,
           "g_attn": gates0, "gate_ssm": gates0 + d_attn, "gate_attn": gates0 + d_attn + d_model}
    return tile_cols, off, dt0, kv0 + kv_lora


def kernel(x, norm_in_w, w_in, conv_w, conv_b, dt_bias, a_log, d_skip, ssm_norm_w, q_norm_w, w_uq,
           kv_norm_w, w_ukv, w_branch_ssm, w_branch_attn, w_out, norm_final_w):
    batch, seq, d_model = x.shape
    depth = w_in.shape[0]
    m = batch * seq
    d_ssm = w_branch_ssm.shape[1]
    d_attn = w_branch_attn.shape[1]
    heads_ssm = dt_bias.shape[1]
    groups = (conv_w.shape[2] - d_ssm) // (2 * SSM_STATE)
    gw = (heads_ssm // groups) * SSM_HEADDIM
    q_lora, kv_lora = w_uq.shape[1], w_ukv.shape[1]
    heads = d_attn // V_HEAD
    assert heads_ssm * SSM_HEADDIM == d_ssm and w_uq.shape[2] == heads * (QK_NOPE + QK_ROPE)

    bm_norm = min(256, m)
    bm_in, bn_in = min(1024, m), 1024
    cps_ssd = 4
    bm_qkv, hb_qkv = min(512, seq), 8
    bq_attn, bk_attn, hpb_attn = min(2048, seq), min(512, seq // 2), 1
    bm_merge, bn_merge = min(512, m), 512
    bm_out, bn_out = min(512, m), 1024

    cs1, cs2 = _rope_tables(seq)
    wt_in = jnp.swapaxes(w_in, 1, 2)
    tile_cols, off, dt_col, kr_col = _in_proj_layout(w_in.shape[2], d_ssm, groups, heads_ssm, q_lora, kv_lora,
                                                     d_attn, d_model, bn_in)
    piece = math.gcd(bn_in, *tile_cols[1:])
    off = {name: col + bn_in for name, col in off.items()}
    h = x.reshape(m, d_model)
    for layer in range(depth):
        u = _rmsnorm(h, norm_in_w[layer], BF16, bm_norm, "norm_in")
        proj = _matmul_wt(u, wt_in, layer, tile_cols, piece, BF16, bm_in, bn_in, "in_proj")
        dt, acs, acst, kr = _dt_prep(u, wt_in, layer, dt_col, kr_col, dt_bias[layer].astype(F32),
                                     a_log[layer].astype(F32), cps_ssd)
        y_ssm = _ssd(proj, off, dt, acs, acst, conv_w[layer], conv_b[layer], d_skip[layer],
                     ssm_norm_w[layer], batch, seq, groups, d_ssm, cps_ssd)

        q = _q_proj(proj, off["cq"], q_norm_w[layer], w_uq[layer], cs1, cs2, seq, heads, bm_qkv, hb_qkv)
        k, v = _kv_proj(proj, off["kvc"], kr, kv_norm_w[layer], w_ukv[layer], cs1, cs2, seq, heads, bm_qkv, hb_qkv)
        y_attn = _attention(q, k, v, proj, off["g_attn"], batch, seq, heads, bq_attn, bk_attn, hpb_attn)

        merged = _merge(y_ssm, y_attn, w_branch_ssm[layer].astype(BF16), w_branch_attn[layer].astype(BF16),
                        proj, off, bm_merge, bn_merge)
        last = layer == depth - 1
        h = _out_proj(merged, w_out[layer].astype(BF16), h, norm_final_w, last, bm_out, bn_out)
    return h.reshape(batch, seq, d_model)
```
